```python
import jax, jax.numpy as jnp
from jax import lax
import numpy as np

D_MODEL = 2048
BATCH = 1
SEQ = 16384
DEPTH = 2

HGRN_DK = 128
HGRN_DV = 128
HGRN_HEADS = D_MODEL // (2 * HGRN_DV)
GDN_DK = 128
GDN_DV = 128
GDN_HEADS = D_MODEL // (2 * GDN_DV)
CONV_W = 4
CHUNK = 64
HGRN_KW = HGRN_HEADS * HGRN_DK
HGRN_VW = HGRN_HEADS * HGRN_DV
GDN_KW = GDN_HEADS * GDN_DK
GDN_VW = GDN_HEADS * GDN_DV
MIX_W = HGRN_VW + GDN_VW
IN_SPLITS = (HGRN_KW, HGRN_KW, HGRN_VW, HGRN_VW, GDN_KW, GDN_KW, GDN_VW, GDN_HEADS, GDN_HEADS, GDN_VW)
IN_W = sum(IN_SPLITS)
SB_HEADS = 16
SB_DH = D_MODEL // SB_HEADS
Q_BLOCK = 128
D_FF = 4 * D_MODEL
EPS = 1e-6
N_A_LAYERS = (DEPTH + 1) // 2
N_C_LAYERS = DEPTH // 2

kernel_name = "hgrn2_gdn_stickbreaking_hybrid"


def rmsnorm(x, gain):
    x32 = x.astype(jnp.float32)
    y = x32 * lax.rsqrt(jnp.mean(x32 * x32, axis=-1, keepdims=True) + EPS)
    return (y * gain.astype(jnp.float32)).astype(x.dtype)


def gated_rmsnorm(o, gate, weight):
    y = o * lax.rsqrt(jnp.mean(o * o, axis=-1, keepdims=True) + EPS)
    return y * weight.astype(jnp.float32) * jax.nn.silu(gate)


def to_heads(t, n_heads):
    return t.reshape(t.shape[0], t.shape[1], n_heads, -1).transpose(0, 2, 1, 3)


def l2norm(t):
    return t * lax.rsqrt(jnp.sum(t * t, axis=-1, keepdims=True) + EPS)


def causal_conv(u, w):
    T = u.shape[1]
    up = jnp.pad(u, ((0, 0), (CONV_W - 1, 0), (0, 0)))
    return sum(up[:, j:j + T] * w[j] for j in range(CONV_W))


def hgrn2_chunked(q, k, v, log_f):
    B, H, T, dk = q.shape
    dv = v.shape[-1]
    n = T // CHUNK

    def split(t):
        return jnp.moveaxis(t.reshape(B, H, n, CHUNK, t.shape[-1]), 2, 0)

    qc, kc, vc = split(q), split(k), split(v)
    bc = lax.cumsum(split(log_f), axis=3)
    causal = jnp.tril(jnp.ones((CHUNK, CHUNK), dtype=bool))

    def step(S, inp):
        qi, ki, vi, bi = inp
        diff = bi[:, :, :, None, :] - bi[:, :, None, :, :]
        decay = jnp.exp(jnp.where(causal[:, :, None], diff, -jnp.inf))
        attn = jnp.einsum('bhtc,bhsc,bhtsc->bhts', qi, ki, decay)
        o = attn @ vi + jnp.einsum('bhtc,bhcv->bhtv', qi * jnp.exp(bi), S)
        b_last = bi[:, :, -1]
        S = jnp.exp(b_last)[..., None] * S + jnp.einsum(
            'bhsc,bhsv->bhcv', ki * jnp.exp(b_last[:, :, None] - bi), vi)
        return S, o

    S0 = jnp.zeros((B, H, dk, dv), jnp.float32)
    _, o = lax.scan(step, S0, (qc, kc, vc, bc))
    return jnp.moveaxis(o, 0, 2).reshape(B, H, T, dv)


def gated_delta_chunked(q, k, v, g, beta):
    B, H, T, dk = q.shape
    dv = v.shape[-1]
    n = T // CHUNK
    q = q.reshape(B, H, n, CHUNK, dk)
    k = k.reshape(B, H, n, CHUNK, dk)
    v = v.reshape(B, H, n, CHUNK, dv)
    g = g.reshape(B, H, n, CHUNK)
    beta = beta.reshape(B, H, n, CHUNK)
    gc = lax.cumsum(g, axis=3)
    tril = jnp.tril(jnp.ones((CHUNK, CHUNK), dtype=bool))
    strict = jnp.tril(jnp.ones((CHUNK, CHUNK), dtype=bool), -1)
    decay = jnp.exp(jnp.where(tril, gc[..., :, None] - gc[..., None, :], -jnp.inf))
    kb = k * beta[..., None]
    m = jnp.where(strict, jnp.einsum('bhntc,bhnsc->bhnts', kb, k) * decay, 0.0)
    eye = jnp.eye(CHUNK, dtype=m.dtype)
    rhs = jnp.concatenate([v * beta[..., None], kb * jnp.exp(gc)[..., None]], axis=-1)
    sol = lax.linalg.triangular_solve(m + eye, rhs, left_side=True, lower=True, unit_diagonal=True)
    u, w = sol[..., :dv], sol[..., dv:]
    qk = jnp.einsum('bhntc,bhnsc->bhnts', q, k) * decay
    q_dec = q * jnp.exp(gc)[..., None]
    g_last = gc[..., -1]
    k_dec = k * jnp.exp(g_last[..., None] - gc)[..., None]

    def step(S, inp):
        ui, wi, qi, qki, ki, gl = inp
        v_new = ui - wi @ S
        o = qi @ S + qki @ v_new
        S = jnp.exp(gl)[..., None, None] * S + jnp.einsum('bhsc,bhsv->bhcv', ki, v_new)
        return S, o

    xs = (jnp.moveaxis(u, 2, 0), jnp.moveaxis(w, 2, 0), jnp.moveaxis(q_dec, 2, 0),
          jnp.moveaxis(qk, 2, 0), jnp.moveaxis(k_dec, 2, 0), jnp.moveaxis(g_last, 2, 0))
    S0 = jnp.zeros((B, H, dk, dv), jnp.float32)
    _, o = lax.scan(step, S0, xs)
    return jnp.moveaxis(o, 0, 2).reshape(B, H, T, dv)


def hgrn2_gdn_mixer(h, w_in, conv_w, a_log, dt_bias, lb, hgrn_norm, gdn_norm, w_out):
    B, T, _ = h.shape
    proj = (h @ w_in).astype(jnp.float32)
    cuts = np.cumsum(IN_SPLITS)[:-1].tolist()
    hq, hf, hi, hg, gq, gk, gv, ga, gb, gg = jnp.split(proj, cuts, axis=-1)

    lb = lb.astype(jnp.float32)
    f = lb + (1.0 - lb) * jax.nn.sigmoid(hf)
    log_f = jnp.log(f)
    k_a = (1.0 - lb) * jax.nn.sigmoid(-hf)
    q_a = jax.nn.silu(hq)
    o_a = hgrn2_chunked(to_heads(q_a, HGRN_HEADS), to_heads(k_a, HGRN_HEADS),
                        to_heads(hi, HGRN_HEADS), to_heads(log_f, HGRN_HEADS))
    o_a = gated_rmsnorm(o_a.transpose(0, 2, 1, 3), hg.reshape(B, T, HGRN_HEADS, HGRN_DV),
                        hgrn_norm).reshape(B, T, HGRN_VW)

    qkv = jax.nn.silu(causal_conv(jnp.concatenate([gq, gk, gv], axis=-1), conv_w.astype(jnp.float32)))
    cq, ck, cv = jnp.split(qkv, [GDN_KW, 2 * GDN_KW], axis=-1)
    q_b = l2norm(to_heads(cq, GDN_HEADS)) * (GDN_DK ** -0.5)
    k_b = l2norm(to_heads(ck, GDN_HEADS))
    g = -jnp.exp(a_log.astype(jnp.float32)) * jax.nn.softplus(ga + dt_bias.astype(jnp.float32))
    beta = jax.nn.sigmoid(gb)
    o_b = gated_delta_chunked(q_b, k_b, to_heads(cv, GDN_HEADS),
                              g.transpose(0, 2, 1), beta.transpose(0, 2, 1))
    o_b = gated_rmsnorm(o_b.transpose(0, 2, 1, 3), gg.reshape(B, T, GDN_HEADS, GDN_DV),
                        gdn_norm).reshape(B, T, GDN_VW)

    o = jnp.concatenate([o_a, o_b], axis=-1).astype(h.dtype)
    return o @ w_out


def stick_breaking_mixer(h, w_qkv, w_o):
    B, T, _ = h.shape
    qkv = (h @ w_qkv).astype(jnp.float32)
    q, k, v = jnp.split(qkv, 3, axis=-1)
    q, k, v = to_heads(q, SB_HEADS), to_heads(k, SB_HEADS), to_heads(v, SB_HEADS)
    nb = T // Q_BLOCK
    q_blocks = jnp.moveaxis(q.reshape(B, SB_HEADS, nb, Q_BLOCK, SB_DH), 2, 0)
    starts = jnp.arange(nb, dtype=jnp.int32) * Q_BLOCK
    k_pos = jnp.arange(T, dtype=jnp.int32)
    q_off = jnp.arange(Q_BLOCK, dtype=jnp.int32)
    scale = SB_DH ** -0.5

    def block(args):
        qb, start = args
        z = jnp.einsum('bhqd,bhkd->bhqk', qb, k) * scale
        mask = k_pos[None, :] < (start + q_off)[:, None]
        log_1m = jnp.where(mask, -jax.nn.softplus(z), 0.0)
        after = lax.cumsum(log_1m, axis=3, reverse=True) - log_1m
        a = jnp.where(mask, jnp.exp(jax.nn.log_sigmoid(z) + after), 0.0)
        return jnp.einsum('bhqk,bhkd->bhqd', a, v)

    o = lax.map(block, (q_blocks, starts))
    o = jnp.moveaxis(o, 0, 2).reshape(B, SB_HEADS, T, SB_DH).transpose(0, 2, 1, 3)
    return o.reshape(B, T, SB_HEADS * SB_DH).astype(h.dtype) @ w_o


def sqrelu_mlp(h, w1, w2):
    return jnp.square(jax.nn.relu(h @ w1)) @ w2


def setup_inputs(seed: int = 0) -> dict:
    key = jax.random.key(seed)
    ks = jax.random.split(key, 20)

    def dense(k, shape, fan_in):
        return jax.random.normal(k, shape, jnp.float32) * (fan_in ** -0.5)

    def gain(k, shape):
        return 1.0 + 0.02 * jax.random.normal(k, shape, jnp.float32)

    dt = jnp.exp(jax.random.uniform(ks[5], (N_A_LAYERS, GDN_HEADS), jnp.float32,
                                    minval=float(np.log(1e-3)), maxval=float(np.log(1e-1))))
    return {
        "x": jax.random.normal(ks[0], (BATCH, SEQ, D_MODEL), jnp.float32),
        "mix_norm": gain(ks[1], (DEPTH, D_MODEL)),
        "a_w_in": dense(ks[2], (N_A_LAYERS, D_MODEL, IN_W), D_MODEL),
        "a_conv_w": dense(ks[3], (N_A_LAYERS, CONV_W, 2 * GDN_KW + GDN_VW), CONV_W),
        "a_a_log": jnp.log(jax.random.uniform(ks[4], (N_A_LAYERS, GDN_HEADS), jnp.float32,
                                              minval=1.0, maxval=16.0)),
        "a_dt_bias": dt + jnp.log(-jnp.expm1(-dt)),
        "a_lb_logits": 0.1 * jax.random.normal(ks[6], (N_A_LAYERS + 1, HGRN_KW), jnp.float32),
        "a_hgrn_norm": gain(ks[7], (N_A_LAYERS, HGRN_DV)),
        "a_gdn_norm": gain(ks[8], (N_A_LAYERS, GDN_DV)),
        "a_w_out": dense(ks[9], (N_A_LAYERS, MIX_W, D_MODEL), MIX_W),
        "c_w_qkv": dense(ks[10], (N_C_LAYERS, D_MODEL, 3 * SB_HEADS * SB_DH), D_MODEL),
        "c_w_o": dense(ks[11], (N_C_LAYERS, SB_HEADS * SB_DH, D_MODEL), SB_HEADS * SB_DH),
        "mlp_norm": gain(ks[12], (DEPTH, D_MODEL)),
        "mlp_w1": dense(ks[13], (DEPTH, D_MODEL, D_FF), D_MODEL),
        "mlp_w2": dense(ks[14], (DEPTH, D_FF, D_MODEL), D_FF),
        "final_norm": gain(ks[15], (D_MODEL,)),
    }


def reference(x, mix_norm, a_w_in, a_conv_w, a_a_log, a_dt_bias, a_lb_logits, a_hgrn_norm,
              a_gdn_norm, a_w_out, c_w_qkv, c_w_o, mlp_norm, mlp_w1, mlp_w2, final_norm):
    lb_all = lax.cumsum(jax.nn.softmax(a_lb_logits.astype(jnp.float32), axis=0), axis=0)
    for layer in range(DEPTH):
        j = layer // 2
        h = rmsnorm(x, mix_norm[layer])
        if layer % 2 == 0:
            x = x + hgrn2_gdn_mixer(h, a_w_in[j], a_conv_w[j], a_a_log[j], a_dt_bias[j],
                                    lb_all[j], a_hgrn_norm[j], a_gdn_norm[j], a_w_out[j])
        else:
            x = x + stick_breaking_mixer(h, c_w_qkv[j], c_w_o[j])
        h = rmsnorm(x, mlp_norm[layer])
        x = x + sqrelu_mlp(h, mlp_w1[layer], mlp_w2[layer])
    return rmsnorm(x, final_norm)
```

```python
import functools

import jax
import jax.numpy as jnp
from jax import lax
from jax.experimental import pallas as pl
from jax.experimental.pallas import tpu as pltpu

F32 = jnp.float32
BF16 = jnp.bfloat16
EPS = 1e-6
HEAD_W = 128
CHUNK = 64
SUB = 16
CONV_W = 4
N_REC_HEADS = 8
VMEM_LIMIT = 48 * 1024 * 1024
HI = lax.Precision.HIGHEST
EXP_ZERO_BELOW = -104.0


def _cparams(sem):
    return pltpu.CompilerParams(dimension_semantics=sem, vmem_limit_bytes=VMEM_LIMIT)


def _dot(a, b, precision=None):
    return jnp.dot(a, b, preferred_element_type=F32, precision=precision)


def _dot_nt(a, b, precision=None):
    return lax.dot_general(a, b, (((1,), (1,)), ((), ())),
                           preferred_element_type=F32, precision=precision)


def _rms(x, gain):
    return x * lax.rsqrt(jnp.mean(x * x, axis=-1, keepdims=True) + EPS) * gain


def _sigmoid(x):
    return 1.0 / (1.0 + jnp.exp(-x))


def _silu(x):
    return x * _sigmoid(x)


def _softplus(x):
    return jnp.maximum(x, 0.0) + jnp.log(1.0 + jnp.exp(-jnp.abs(x)))


def _norm_matmul_kernel(x_ref, g_ref, w_ref, cs_ref, o_ref, h_ref):
    @pl.when(pl.program_id(1) == 0)
    def _():
        h_ref[...] = _rms(x_ref[...], g_ref[...]).astype(BF16)

    o_ref[...] = (_dot(h_ref[...], w_ref[...]) * cs_ref[...]).astype(o_ref.dtype)


def norm_matmul(x, gain, w, col_scale, out_dtype, tm, tn):
    m, k = x.shape
    n = w.shape[1]
    return pl.pallas_call(
        _norm_matmul_kernel,
        grid=(m // tm, n // tn),
        in_specs=[
            pl.BlockSpec((tm, k), lambda i, j: (i, 0)),
            pl.BlockSpec((1, k), lambda i, j: (0, 0)),
            pl.BlockSpec((k, tn), lambda i, j: (0, j)),
            pl.BlockSpec((1, tn), lambda i, j: (0, j)),
        ],
        out_specs=pl.BlockSpec((tm, tn), lambda i, j: (i, j)),
        out_shape=jax.ShapeDtypeStruct((m, n), out_dtype),
        scratch_shapes=[pltpu.VMEM((tm, k), BF16)],
        compiler_params=_cparams(("parallel", "arbitrary")),
        name="norm_matmul",
    )(x, gain.reshape(1, k), w, col_scale.reshape(1, n))


def _proj_residual_kernel(*refs, n_lhs):
    lhs = refs[:n_lhs]
    ws = refs[n_lhs:2 * n_lhs]
    x_ref, o_ref = refs[2 * n_lhs], refs[2 * n_lhs + 1]
    acc = x_ref[...]
    for a_ref, w_ref in zip(lhs, ws):
        acc = acc + _dot(a_ref[...], w_ref[...])
    o_ref[...] = acc


def proj_residual(lhs_list, w_list, x, tm, tn):
    m, n = x.shape
    n_lhs = len(lhs_list)
    in_specs = [pl.BlockSpec((tm, a.shape[1]), lambda i, j: (i, 0)) for a in lhs_list]
    in_specs += [pl.BlockSpec((w.shape[0], tn), lambda i, j: (0, j)) for w in w_list]
    in_specs += [pl.BlockSpec((tm, tn), lambda i, j: (i, j))]
    return pl.pallas_call(
        functools.partial(_proj_residual_kernel, n_lhs=n_lhs),
        grid=(m // tm, n // tn),
        in_specs=in_specs,
        out_specs=pl.BlockSpec((tm, tn), lambda i, j: (i, j)),
        out_shape=jax.ShapeDtypeStruct((m, n), F32),
        compiler_params=_cparams(("parallel", "arbitrary")),
        name="proj_residual",
    )(*lhs_list, *w_list, x)


def _mlp_kernel(x_ref, g_ref, w1_ref, w2_ref, fg_ref, o_ref, h_ref, acc_ref, *, final_norm):
    j = pl.program_id(1)

    @pl.when(j == 0)
    def _():
        h_ref[...] = _rms(x_ref[...], g_ref[...]).astype(BF16)
        acc_ref[...] = jnp.zeros_like(acc_ref)

    a = _dot(h_ref[...], w1_ref[...])
    a = jnp.square(jnp.maximum(a, 0.0)).astype(BF16)
    acc_ref[...] += _dot(a, w2_ref[...])

    @pl.when(j == pl.num_programs(1) - 1)
    def _():
        y = x_ref[...] + acc_ref[...]
        if final_norm:
            y = _rms(y, fg_ref[...])
        o_ref[...] = y


def mlp_residual(x, gain, w1, w2, final_gain, final_norm, tm, tf):
    m, d = x.shape
    ff = w1.shape[1]
    return pl.pallas_call(
        functools.partial(_mlp_kernel, final_norm=final_norm),
        grid=(m // tm, ff // tf),
        in_specs=[
            pl.BlockSpec((tm, d), lambda i, j: (i, 0)),
            pl.BlockSpec((1, d), lambda i, j: (0, 0)),
            pl.BlockSpec((d, tf), lambda i, j: (0, j)),
            pl.BlockSpec((tf, d), lambda i, j: (j, 0)),
            pl.BlockSpec((1, d), lambda i, j: (0, 0)),
        ],
        out_specs=pl.BlockSpec((tm, d), lambda i, j: (i, 0)),
        out_shape=jax.ShapeDtypeStruct((m, d), F32),
        scratch_shapes=[pltpu.VMEM((tm, d), BF16), pltpu.VMEM((tm, d), F32)],
        compiler_params=_cparams(("parallel", "arbitrary")),
        name="mlp_residual",
    )(x, gain.reshape(1, d), w1, w2, final_gain.reshape(1, d))


def _hgrn_kernel(hq_ref, hf_ref, hi_ref, hg_ref, lb_ref, nw_ref, o_ref, st_ref, *, n_chunks):
    @pl.when(pl.program_id(1) == 0)
    def _():
        st_ref[...] = jnp.zeros_like(st_ref)

    lb = lb_ref[...]
    row = lax.broadcasted_iota(jnp.int32, (CHUNK, CHUNK), 0)
    col = lax.broadcasted_iota(jnp.int32, (CHUNK, CHUNK), 1)
    tril = (col <= row).astype(F32)
    srow = lax.broadcasted_iota(jnp.int32, (SUB, SUB), 0)
    scol = lax.broadcasted_iota(jnp.int32, (SUB, SUB), 1)
    sub_causal = scol <= srow

    for c in range(n_chunks):
        rows = pl.ds(c * CHUNK, CHUNK)
        hf = hf_ref[rows, :]
        f = lb + (1.0 - lb) * _sigmoid(hf)
        k = (1.0 - lb) * _sigmoid(-hf)
        q = _silu(hq_ref[rows, :])
        v = hi_ref[rows, :]
        vb = v.astype(BF16)
        b = _dot(tril, jnp.log(f), precision=HI)
        st = st_ref[...]
        o_inter = _dot_nt((q * jnp.exp(b)).astype(BF16), st.astype(BF16))
        outs = []
        for i in range(CHUNK // SUB):
            lo, hi = i * SUB, (i + 1) * SUB
            b_ref = b[lo:lo + 1, :]
            qs = (q[lo:hi] * jnp.exp(b[lo:hi] - b_ref)).astype(BF16)
            ks = (k[:hi] * jnp.exp(b_ref - b[:hi])).astype(BF16)
            att = _dot_nt(qs, ks)
            diag = jnp.where(sub_causal, att[:, lo:hi], 0.0)
            o_sub = _dot(diag.astype(BF16), vb[lo:hi])
            if i > 0:
                o_sub = o_sub + _dot(att[:, :lo].astype(BF16), vb[:lo])
            outs.append(o_sub)
        o = jnp.concatenate(outs, axis=0) + o_inter
        b_last = b[CHUNK - 1:CHUNK, :]
        kd = (k * jnp.exp(b_last - b)).astype(BF16)
        st_ref[...] = st * jnp.exp(b_last) + _dot(vb.T, kd)
        y = o * lax.rsqrt(jnp.mean(o * o, axis=-1, keepdims=True) + EPS)
        o_ref[rows, :] = (y * nw_ref[...] * _silu(hg_ref[rows, :])).astype(o_ref.dtype)


def hgrn_heads(proj, lb, norm_w, tb):
    t = proj.shape[0]
    nh = N_REC_HEADS

    def col(block0):
        return pl.BlockSpec((tb, HEAD_W), lambda h, i: (i, block0 + h))

    return pl.pallas_call(
        functools.partial(_hgrn_kernel, n_chunks=tb // CHUNK),
        grid=(nh, t // tb),
        in_specs=[col(0), col(nh), col(2 * nh), col(3 * nh),
                  pl.BlockSpec((None, 1, HEAD_W), lambda h, i: (h, 0, 0)),
                  pl.BlockSpec((1, HEAD_W), lambda h, i: (0, 0))],
        out_specs=pl.BlockSpec((tb, HEAD_W), lambda h, i: (i, h)),
        out_shape=jax.ShapeDtypeStruct((t, nh * HEAD_W), BF16),
        scratch_shapes=[pltpu.VMEM((HEAD_W, HEAD_W), F32)],
        compiler_params=_cparams(("parallel", "arbitrary")),
        name="hgrn_heads",
    )(proj, proj, proj, proj, lb.reshape(nh, 1, HEAD_W), norm_w.reshape(1, HEAD_W))


def _gdn_kernel(alog_ref, dtb_ref, gq_ref, gk_ref, gv_ref, gg_ref, ga_ref, gb_ref,
                cwq_ref, cwk_ref, cwv_ref, nw_ref, o_ref, s_ref, tail_ref, *, n_chunks):
    h = pl.program_id(0)
    tb = n_chunks * CHUNK

    @pl.when(pl.program_id(1) == 0)
    def _():
        s_ref[...] = jnp.zeros_like(s_ref)
        tail_ref[...] = jnp.zeros_like(tail_ref)

    def conv_silu(u_ref, w_ref, slot):
        u = u_ref[...]
        ext = jnp.concatenate([tail_ref[slot], u], axis=0)
        w = w_ref[...]
        acc = u * w[CONV_W - 1:CONV_W, :]
        for j in range(CONV_W - 1):
            shifted = pltpu.roll(ext, CONV_W - 1 - j, axis=0)[8:, :]
            acc = acc + shifted * w[j:j + 1, :]
        tail_ref[slot] = u[tb - 8:, :]
        return _silu(acc)

    cq = conv_silu(gq_ref, cwq_ref, 0)
    ck = conv_silu(gk_ref, cwk_ref, 1)
    cv = conv_silu(gv_ref, cwv_ref, 2)
    q_all = cq * lax.rsqrt(jnp.sum(cq * cq, axis=-1, keepdims=True) + EPS) * (HEAD_W ** -0.5)
    k_all = ck * lax.rsqrt(jnp.sum(ck * ck, axis=-1, keepdims=True) + EPS)

    neg_a = -jnp.exp(jnp.full((1, tb), alog_ref[h], F32))
    g_all = neg_a * _softplus(ga_ref[...] + dtb_ref[h])
    beta_all = _sigmoid(gb_ref[...])

    row = lax.broadcasted_iota(jnp.int32, (CHUNK, CHUNK), 0)
    col = lax.broadcasted_iota(jnp.int32, (CHUNK, CHUNK), 1)
    tril = col <= row
    strict = col < row
    eye = col == row
    eye_f = eye.astype(F32)
    upper_f = (row <= col).astype(F32)

    for c in range(n_chunks):
        lo, hi = c * CHUNK, (c + 1) * CHUNK
        q, k, v = q_all[lo:hi], k_all[lo:hi], cv[lo:hi]
        g_b = jnp.broadcast_to(g_all[:, lo:hi], (CHUNK, CHUNK))
        beta_b = jnp.broadcast_to(beta_all[:, lo:hi], (CHUNK, CHUNK))
        gc_row = _dot(g_b, upper_f, precision=HI)
        gc_col = jnp.sum(jnp.where(tril, g_b, 0.0), axis=-1, keepdims=True)
        beta_col = jnp.sum(jnp.where(eye, beta_b, 0.0), axis=-1, keepdims=True)
        decay = jnp.where(tril, jnp.exp(jnp.minimum(gc_col - gc_row, 0.0)), 0.0)
        kb = k * beta_col
        n1 = jnp.where(strict, -_dot_nt(kb, k, precision=HI) * decay, 0.0)
        inv = eye_f + n1
        npow = n1
        for _ in range(5):
            npow = _dot(npow, npow, precision=HI)
            inv = inv + _dot(inv, npow, precision=HI)
        e_gc = jnp.exp(gc_col)
        u = _dot(inv, v * beta_col, precision=HI)
        w = _dot(inv, kb * e_gc, precision=HI)
        qk = jnp.where(tril, _dot_nt(q.astype(BF16), k.astype(BF16)) * decay, 0.0)
        g_last = gc_col[CHUNK - 1:CHUNK, :]
        q_dec = (q * e_gc).astype(BF16)
        k_dec = (k * jnp.exp(g_last - gc_col)).astype(BF16)
        s = s_ref[...]
        sb = s.astype(BF16)
        v_new = u - _dot(w.astype(BF16), sb)
        v_new_b = v_new.astype(BF16)
        o = _dot(q_dec, sb) + _dot(qk.astype(BF16), v_new_b)
        s_ref[...] = s * jnp.exp(g_last) + _dot(k_dec.T, v_new_b)
        y = o * lax.rsqrt(jnp.mean(o * o, axis=-1, keepdims=True) + EPS)
        o_ref[lo:hi, :] = (y * nw_ref[...] * _silu(gg_ref[lo:hi, :])).astype(o_ref.dtype)


def gdn_heads(proj, gab_t, conv_w, a_log, dt_bias, norm_w, tb):
    t = proj.shape[0]
    nh = N_REC_HEADS

    def col(block0):
        return pl.BlockSpec((tb, HEAD_W), lambda h, i: (i, block0 + h))

    def gate_row(block0):
        return pl.BlockSpec((None, 1, tb), lambda h, i: (block0 + h, 0, i))

    def conv_col(block0):
        return pl.BlockSpec((CONV_W, HEAD_W), lambda h, i: (0, block0 + h))

    smem = pl.BlockSpec(memory_space=pltpu.SMEM)
    return pl.pallas_call(
        functools.partial(_gdn_kernel, n_chunks=tb // CHUNK),
        grid=(nh, t // tb),
        in_specs=[smem, smem, col(4 * nh), col(5 * nh), col(6 * nh), col(7 * nh),
                  gate_row(0), gate_row(nh),
                  conv_col(0), conv_col(nh), conv_col(2 * nh),
                  pl.BlockSpec((1, HEAD_W), lambda h, i: (0, 0))],
        out_specs=pl.BlockSpec((tb, HEAD_W), lambda h, i: (i, h)),
        out_shape=jax.ShapeDtypeStruct((t, nh * HEAD_W), BF16),
        scratch_shapes=[pltpu.VMEM((HEAD_W, HEAD_W), F32),
                        pltpu.VMEM((3, 8, HEAD_W), F32)],
        compiler_params=_cparams(("parallel", "arbitrary")),
        name="gdn_heads",
    )(a_log, dt_bias, proj, proj, proj, proj, gab_t, gab_t,
      conv_w, conv_w, conv_w, norm_w.reshape(1, HEAD_W))


def _sb_kernel(q_ref, k_ref, v_ref, o_ref, acc_ref, c_ref, *, blk):
    qi = pl.program_id(1)
    q = q_ref[...]
    row = lax.broadcasted_iota(jnp.int32, (blk, blk), 0)
    col = lax.broadcasted_iota(jnp.int32, (blk, blk), 1)
    later = (row > col).astype(BF16)
    causal = col < row

    def key_block(j, masked):
        start = pl.multiple_of(j * blk, blk)
        kj = k_ref[pl.ds(start, blk), :]
        vj = v_ref[pl.ds(start, blk), :]
        z = _dot_nt(q, kj)
        sp = _softplus(z)
        log_1m = -sp
        if masked:
            log_1m = jnp.where(causal, log_1m, 0.0)
        l_hi = log_1m.astype(BF16)
        l_lo = (log_1m - l_hi.astype(F32)).astype(BF16)
        after = _dot(l_hi, later) + _dot(l_lo, later)
        c = c_ref[...]
        a = jnp.exp((z - sp) + after + c)
        if masked:
            a = jnp.where(causal, a, 0.0)
        acc_ref[...] += _dot(a.astype(BF16), vj)
        c_new = c + after[:, :1] + log_1m[:, :1]
        c_ref[...] = c_new
        return jnp.max(c_new)

    acc_ref[...] = jnp.zeros_like(acc_ref)
    c_ref[...] = jnp.zeros_like(c_ref)
    c_max = key_block(qi, True)

    def cond(carry):
        j, c_max = carry
        return jnp.logical_and(j >= 0, c_max > EXP_ZERO_BELOW)

    def body(carry):
        j, _ = carry
        return j - 1, key_block(j, False)

    lax.while_loop(cond, body, (qi - 1, c_max))
    o_ref[...] = acc_ref[...].astype(o_ref.dtype)


def sb_attention(qkv, n_heads, blk):
    t = qkv.shape[0]
    return pl.pallas_call(
        functools.partial(_sb_kernel, blk=blk),
        grid=(n_heads, t // blk),
        in_specs=[pl.BlockSpec((blk, HEAD_W), lambda h, i: (i, h)),
                  pl.BlockSpec((t, HEAD_W), lambda h, i: (0, n_heads + h)),
                  pl.BlockSpec((t, HEAD_W), lambda h, i: (0, 2 * n_heads + h))],
        out_specs=pl.BlockSpec((blk, HEAD_W), lambda h, i: (i, h)),
        out_shape=jax.ShapeDtypeStruct((t, n_heads * HEAD_W), BF16),
        scratch_shapes=[pltpu.VMEM((blk, HEAD_W), F32), pltpu.VMEM((blk, 1), F32)],
        compiler_params=_cparams(("parallel", "arbitrary")),
        name="sb_attention",
    )(qkv, qkv, qkv)


def _pick(n, candidates):
    for c in candidates:
        if n % c == 0:
            return c
    return n


def kernel(x, mix_norm, a_w_in, a_conv_w, a_a_log, a_dt_bias, a_lb_logits, a_hgrn_norm,
           a_gdn_norm, a_w_out, c_w_qkv, c_w_o, mlp_norm, mlp_w1, mlp_w2, final_norm):
    bsz, t, d = x.shape
    assert bsz == 1
    depth = mix_norm.shape[0]
    nh = N_REC_HEADS
    kw = nh * HEAD_W
    xs = x.reshape(t, d)
    tm = _pick(t, (512, 256, 128, 64))
    tb = _pick(t, (256, 128, 64))
    sb_heads = d // HEAD_W
    sb_blk = _pick(t, (256, 128))

    lb_all = jnp.cumsum(jax.nn.softmax(a_lb_logits.astype(F32), axis=0), axis=0)
    for layer in range(depth):
        j = layer // 2
        if layer % 2 == 0:
            w_in = a_w_in[j]
            n_gate = 2 * nh
            w_main = jnp.concatenate([w_in[:, :7 * kw], w_in[:, 7 * kw + n_gate:]], axis=1).astype(BF16)
            w_gate = jnp.pad(w_in[:, 7 * kw:7 * kw + n_gate], ((0, 0), (0, HEAD_W - n_gate))).astype(BF16)
            proj = norm_matmul(xs, mix_norm[layer], w_main, jnp.ones((8 * kw,), F32), F32,
                               tm, _pick(8 * kw, (1024,)))
            gates = norm_matmul(xs, mix_norm[layer], w_gate, jnp.ones((HEAD_W,), F32), F32,
                                tm, HEAD_W)
            gab_t = gates[:, :n_gate].T.reshape(n_gate, 1, t)
            o_a = hgrn_heads(proj, lb_all[j], a_hgrn_norm[j], tb)
            o_b = gdn_heads(proj, gab_t, a_conv_w[j], a_a_log[j], a_dt_bias[j], a_gdn_norm[j], tb)
            w_out = a_w_out[j].astype(BF16)
            xs = proj_residual([o_a, o_b], [w_out[:kw], w_out[kw:]], xs, tm, _pick(d, (1024,)))
        else:
            scale = jnp.concatenate([jnp.full((d,), HEAD_W ** -0.5, F32), jnp.ones((2 * d,), F32)])
            qkv = norm_matmul(xs, mix_norm[layer], c_w_qkv[j].astype(BF16), scale, BF16,
                              tm, _pick(3 * d, (1024,)))
            o_c = sb_attention(qkv, sb_heads, sb_blk)
            xs = proj_residual([o_c], [c_w_o[j].astype(BF16)], xs, tm, _pick(d, (1024,)))
        last = layer == depth - 1
        xs = mlp_residual(xs, mlp_norm[layer], mlp_w1[layer].astype(BF16), mlp_w2[layer].astype(BF16),
                          final_norm, last, tm, _pick(mlp_w1.shape[2], (512,)))
    return xs.reshape(bsz, t, d)
```

```python
import functools

import jax
import jax.numpy as jnp
from jax import lax
from jax.experimental import pallas as pl
from jax.experimental.pallas import tpu as pltpu

F32 = jnp.float32
BF16 = jnp.bfloat16
EPS = 1e-6
HEAD_W = 128
CHUNK = 64
SUB = 16
CONV_W = 4
SB_BLK = 128
N_REC_HEADS = 8
WY_GROUP = 4
VMEM_LIMIT = 48 * 1024 * 1024
HI = lax.Precision.HIGHEST
EXP_ZERO_BELOW = -104.0


def _cparams(sem):
    return pltpu.CompilerParams(dimension_semantics=sem, vmem_limit_bytes=VMEM_LIMIT)


def _dot(a, b, precision=None):
    return jnp.dot(a, b, preferred_element_type=F32, precision=precision)


def _dot_nt(a, b, precision=None):
    return lax.dot_general(a, b, (((1,), (1,)), ((), ())),
                           preferred_element_type=F32, precision=precision)


def _rms(x, gain):
    return x * lax.rsqrt(jnp.mean(x * x, axis=-1, keepdims=True) + EPS) * gain


def _sigmoid(x):
    return 0.5 * jnp.tanh(0.5 * x) + 0.5


def _silu(x):
    return x * _sigmoid(x)


def _softplus(x):
    return jnp.maximum(x, 0.0) + jnp.log(1.0 + jnp.exp(-jnp.abs(x)))


def _norm_matmul_kernel(x_ref, g_ref, w_ref, cs_ref, o_ref, h_ref):
    @pl.when(pl.program_id(1) == 0)
    def _():
        h_ref[...] = _rms(x_ref[...], g_ref[...]).astype(BF16)

    o_ref[...] = (_dot(h_ref[...], w_ref[...]) * cs_ref[...]).astype(o_ref.dtype)


def norm_matmul(x, gain, w, col_scale, out_dtype, tm, tn):
    m, k = x.shape
    n = w.shape[1]
    return pl.pallas_call(
        _norm_matmul_kernel,
        grid=(m // tm, n // tn),
        in_specs=[
            pl.BlockSpec((tm, k), lambda i, j: (i, 0)),
            pl.BlockSpec((1, k), lambda i, j: (0, 0)),
            pl.BlockSpec((k, tn), lambda i, j: (0, j)),
            pl.BlockSpec((1, tn), lambda i, j: (0, j)),
        ],
        out_specs=pl.BlockSpec((tm, tn), lambda i, j: (i, j)),
        out_shape=jax.ShapeDtypeStruct((m, n), out_dtype),
        scratch_shapes=[pltpu.VMEM((tm, k), BF16)],
        compiler_params=_cparams(("parallel", "arbitrary")),
        name="norm_matmul",
    )(x, gain.reshape(1, k), w, col_scale.reshape(1, n))


def _proj_residual_kernel(*refs, n_lhs):
    lhs = refs[:n_lhs]
    ws = refs[n_lhs:2 * n_lhs]
    x_ref, o_ref = refs[2 * n_lhs], refs[2 * n_lhs + 1]
    acc = x_ref[...]
    for a_ref, w_ref in zip(lhs, ws):
        acc = acc + _dot(a_ref[...], w_ref[...])
    o_ref[...] = acc


def proj_residual(lhs_list, w_list, x, tm, tn):
    m, n = x.shape
    n_lhs = len(lhs_list)
    in_specs = [pl.BlockSpec((tm, a.shape[1]), lambda i, j: (i, 0)) for a in lhs_list]
    in_specs += [pl.BlockSpec((w.shape[0], tn), lambda i, j: (0, j)) for w in w_list]
    in_specs += [pl.BlockSpec((tm, tn), lambda i, j: (i, j))]
    return pl.pallas_call(
        functools.partial(_proj_residual_kernel, n_lhs=n_lhs),
        grid=(m // tm, n // tn),
        in_specs=in_specs,
        out_specs=pl.BlockSpec((tm, tn), lambda i, j: (i, j)),
        out_shape=jax.ShapeDtypeStruct((m, n), F32),
        compiler_params=_cparams(("parallel", "arbitrary")),
        name="proj_residual",
    )(*lhs_list, *w_list, x)


def _mlp_kernel(x_ref, g_ref, w1_ref, w2_ref, fg_ref, o_ref, h_ref, acc_ref, *, final_norm):
    j = pl.program_id(1)

    @pl.when(j == 0)
    def _():
        h_ref[...] = _rms(x_ref[...], g_ref[...]).astype(BF16)
        acc_ref[...] = jnp.zeros_like(acc_ref)

    a = _dot(h_ref[...], w1_ref[...])
    a = jnp.square(jnp.maximum(a, 0.0)).astype(BF16)
    acc_ref[...] += _dot(a, w2_ref[...])

    @pl.when(j == pl.num_programs(1) - 1)
    def _():
        y = x_ref[...] + acc_ref[...]
        if final_norm:
            y = _rms(y, fg_ref[...])
        o_ref[...] = y


def mlp_residual(x, gain, w1, w2, final_gain, final_norm, tm, tf):
    m, d = x.shape
    ff = w1.shape[1]
    return pl.pallas_call(
        functools.partial(_mlp_kernel, final_norm=final_norm),
        grid=(m // tm, ff // tf),
        in_specs=[
            pl.BlockSpec((tm, d), lambda i, j: (i, 0)),
            pl.BlockSpec((1, d), lambda i, j: (0, 0)),
            pl.BlockSpec((d, tf), lambda i, j: (0, j)),
            pl.BlockSpec((tf, d), lambda i, j: (j, 0)),
            pl.BlockSpec((1, d), lambda i, j: (0, 0)),
        ],
        out_specs=pl.BlockSpec((tm, d), lambda i, j: (i, 0)),
        out_shape=jax.ShapeDtypeStruct((m, d), F32),
        scratch_shapes=[pltpu.VMEM((tm, d), BF16), pltpu.VMEM((tm, d), F32)],
        compiler_params=_cparams(("parallel", "arbitrary")),
        name="mlp_residual",
    )(x, gain.reshape(1, d), w1, w2, final_gain.reshape(1, d))


def _hgrn_kernel(hq_ref, hf_ref, hi_ref, hg_ref, lb_ref, nw_ref, o_ref, st_ref, *, n_chunks):
    @pl.when(pl.program_id(1) == 0)
    def _():
        st_ref[...] = jnp.zeros_like(st_ref)

    chunks = range(n_chunks)
    n_sub = CHUNK // SUB
    lb = lb_ref[...]
    row = lax.broadcasted_iota(jnp.int32, (CHUNK, CHUNK), 0)
    col = lax.broadcasted_iota(jnp.int32, (CHUNK, CHUNK), 1)
    tril = (col <= row).astype(F32)

    hf = hf_ref[...]
    log_f = jnp.log(lb + (1.0 - lb) * _sigmoid(hf))
    k_all = (1.0 - lb) * _sigmoid(-hf)
    q_all = _silu(hq_ref[...])
    vb_all = hi_ref[...].astype(BF16)

    rows = [slice(c * CHUNK, (c + 1) * CHUNK) for c in chunks]
    q = [q_all[r] for r in rows]
    k = [k_all[r] for r in rows]
    vb = [vb_all[r] for r in rows]
    b = [_dot(tril, log_f[r], precision=HI) for r in rows]
    o_intra = [[] for _ in chunks]
    for i in range(n_sub):
        lo, hi = i * SUB, (i + 1) * SUB
        srow = lax.broadcasted_iota(jnp.int32, (SUB, hi), 0)
        scol = lax.broadcasted_iota(jnp.int32, (SUB, hi), 1)
        causal = scol <= srow + lo
        b_ref = [b[c][lo:lo + 1, :] for c in chunks]
        qs = [(q[c][lo:hi] * jnp.exp(b[c][lo:hi] - b_ref[c])).astype(BF16) for c in chunks]
        ks = [(k[c][:hi] * jnp.exp(b_ref[c] - b[c][:hi])).astype(BF16) for c in chunks]
        att = [jnp.where(causal, _dot_nt(qs[c], ks[c]), 0.0).astype(BF16) for c in chunks]
        for c in chunks:
            o_intra[c].append(_dot(att[c], vb[c][:hi]))
    b_last = [b[c][CHUNK - 1:CHUNK, :] for c in chunks]
    qd = [(q[c] * jnp.exp(b[c])).astype(BF16) for c in chunks]
    kd = [(k[c] * jnp.exp(b_last[c] - b[c])).astype(BF16) for c in chunks]
    p = [_dot(vb[c].T, kd[c]) for c in chunks]
    d = [jnp.exp(b_last[c]) for c in chunks]
    gate = nw_ref[...] * _silu(hg_ref[...])

    st = st_ref[...]
    for c in chunks:
        o = jnp.concatenate(o_intra[c], axis=0) + _dot_nt(qd[c], st.astype(BF16))
        st = st * d[c] + p[c]
        y = o * lax.rsqrt(jnp.mean(o * o, axis=-1, keepdims=True) + EPS)
        o_ref[rows[c], :] = (y * gate[rows[c]]).astype(o_ref.dtype)
    st_ref[...] = st


def hgrn_heads(proj, lb, norm_w, tb):
    t = proj.shape[0]
    nh = N_REC_HEADS

    def col(block0):
        return pl.BlockSpec((tb, HEAD_W), lambda h, i: (i, block0 + h))

    return pl.pallas_call(
        functools.partial(_hgrn_kernel, n_chunks=tb // CHUNK),
        grid=(nh, t // tb),
        in_specs=[col(0), col(nh), col(2 * nh), col(3 * nh),
                  pl.BlockSpec((None, 1, HEAD_W), lambda h, i: (h, 0, 0)),
                  pl.BlockSpec((1, HEAD_W), lambda h, i: (0, 0))],
        out_specs=pl.BlockSpec((tb, HEAD_W), lambda h, i: (i, h)),
        out_shape=jax.ShapeDtypeStruct((t, nh * HEAD_W), BF16),
        scratch_shapes=[pltpu.VMEM((HEAD_W, HEAD_W), F32)],
        compiler_params=_cparams(("parallel", "arbitrary")),
        name="hgrn_heads",
    )(proj, proj, proj, proj, lb.reshape(nh, 1, HEAD_W), norm_w.reshape(1, HEAD_W))


def _gdn_kernel(alog_ref, dtb_ref, gq_ref, gk_ref, gv_ref, gg_ref, gab_ref, cw_ref, nw_ref, o_ref,
                s_ref, tail_ref, q_s, k_s, v_s, g_s, beta_s, u_s, wq_s, kdt_s, qk_s, gl_s, *, n_chunks):
    tb = n_chunks * CHUNK
    nh = N_REC_HEADS

    @pl.when(pl.program_id(0) == 0)
    def _():
        s_ref[...] = jnp.zeros_like(s_ref)
        tail_ref[...] = jnp.zeros_like(tail_ref)

    def conv_silu(u_ref, slot, h):
        cols = slice(h * HEAD_W, (h + 1) * HEAD_W)
        u = u_ref[:, cols]
        ext = jnp.concatenate([tail_ref[slot, :, cols], u], axis=0)
        w = cw_ref[:, slot * nh * HEAD_W + h * HEAD_W:slot * nh * HEAD_W + (h + 1) * HEAD_W]
        acc = u * w[CONV_W - 1:CONV_W, :]
        for j in range(CONV_W - 1):
            acc = acc + pltpu.roll(ext, CONV_W - 1 - j, axis=0)[8:, :] * w[j:j + 1, :]
        tail_ref[slot, :, cols] = u[tb - 8:, :]
        return _silu(acc)

    for h in range(nh):
        qh = conv_silu(gq_ref, 0, h)
        kh = conv_silu(gk_ref, 1, h)
        q_s[h] = qh * lax.rsqrt(jnp.sum(qh * qh, axis=-1, keepdims=True) + EPS) * (HEAD_W ** -0.5)
        k_s[h] = kh * lax.rsqrt(jnp.sum(kh * kh, axis=-1, keepdims=True) + EPS)
        v_s[h] = conv_silu(gv_ref, 2, h)
    g = -jnp.exp(alog_ref[...]) * _softplus(gab_ref[:nh, :] + dtb_ref[...])
    brow = lax.broadcasted_iota(jnp.int32, (tb, tb), 0)
    bcol = lax.broadcasted_iota(jnp.int32, (tb, tb), 1)
    same_chunk_upto = jnp.logical_and(brow <= bcol, brow // CHUNK == bcol // CHUNK).astype(F32)
    g_s[...] = _dot(g, same_chunk_upto, precision=HI)
    beta_s[...] = _sigmoid(gab_ref[nh:, :])

    row = lax.broadcasted_iota(jnp.int32, (CHUNK, CHUNK), 0)
    col = lax.broadcasted_iota(jnp.int32, (CHUNK, CHUNK), 1)
    tril = col <= row
    strict = col < row
    eye = col == row
    eye_f = eye.astype(F32)

    def wy_heads(hg, carry):
        items = [(hg * WY_GROUP + dh, c) for dh in range(WY_GROUP) for c in range(n_chunks)]
        n = range(len(items))
        rows = [slice(c * CHUNK, (c + 1) * CHUNK) for _, c in items]
        q = [q_s[h, rows[x], :] for x, (h, _) in enumerate(items)]
        k = [k_s[h, rows[x], :] for x, (h, _) in enumerate(items)]
        gc_all = [g_s[pl.ds(hg * WY_GROUP + dh, 1), :] for dh in range(WY_GROUP)]
        beta_all = [beta_s[pl.ds(hg * WY_GROUP + dh, 1), :] for dh in range(WY_GROUP)]
        gc_row = [jnp.broadcast_to(gc_all[x // n_chunks][:, rows[x]], (CHUNK, CHUNK)) for x in n]
        beta_b = [jnp.broadcast_to(beta_all[x // n_chunks][:, rows[x]], (CHUNK, CHUNK)) for x in n]
        gc_col = [jnp.sum(jnp.where(eye, gc_row[x], 0.0), axis=-1, keepdims=True) for x in n]
        beta_col = [jnp.sum(jnp.where(eye, beta_b[x], 0.0), axis=-1, keepdims=True) for x in n]
        decay = [jnp.where(tril, jnp.exp(jnp.minimum(gc_col[x] - gc_row[x], 0.0)), 0.0) for x in n]
        kb = [k[x] * beta_col[x] for x in n]
        prod = [_dot_nt(jnp.concatenate([kb[x], q[x]], axis=0).astype(BF16), k[x].astype(BF16))
                for x in n]
        n1 = [jnp.where(strict, -prod[x][:CHUNK] * decay[x], 0.0) for x in n]
        inv = [eye_f + n1[x] for x in n]
        npow = [n1[x].astype(BF16) for x in n]
        for _ in range(5):
            npow = [_dot(npow[x], npow[x]).astype(BF16) for x in n]
            inv = [inv[x] + _dot(inv[x].astype(BF16), npow[x]) for x in n]
        e_gc = [jnp.exp(gc_col[x]) for x in n]
        rhs = [jnp.concatenate([v_s[h, rows[x], :] * beta_col[x], kb[x] * e_gc[x]], axis=1)
               for x, (h, _) in enumerate(items)]
        sol = [_dot(inv[x].astype(BF16), rhs[x].astype(BF16)) for x in n]
        for x, (h, c) in enumerate(items):
            g_last = gc_col[x][CHUNK - 1:CHUNK, :]
            u_s[h, rows[x], :] = sol[x][:, :HEAD_W]
            wq_s[h, c, :CHUNK, :] = sol[x][:, HEAD_W:].astype(BF16)
            wq_s[h, c, CHUNK:, :] = (q[x] * e_gc[x]).astype(BF16)
            kdt_s[h, c] = (k[x] * jnp.exp(g_last - gc_col[x])).T.astype(BF16)
            qk_s[h, c] = (prod[x][CHUNK:] * decay[x]).astype(BF16)
            gl_s[h, c] = jnp.broadcast_to(jnp.exp(g_last), (8, HEAD_W))
        return carry

    lax.fori_loop(0, nh // WY_GROUP, wy_heads, 0)

    for c in range(n_chunks):
        lo, hi = c * CHUNK, (c + 1) * CHUNK
        for h in range(nh):
            cols = slice(h * HEAD_W, (h + 1) * HEAD_W)
            s = s_ref[h]
            ws = _dot(wq_s[h, c], s.astype(BF16))
            v_new = (u_s[h, lo:hi, :] - ws[:CHUNK]).astype(BF16)
            o = ws[CHUNK:] + _dot(qk_s[h, c], v_new)
            s_ref[h] = s * gl_s[h, c][:1, :] + _dot(kdt_s[h, c], v_new)
            y = o * lax.rsqrt(jnp.mean(o * o, axis=-1, keepdims=True) + EPS)
            o_ref[lo:hi, cols] = (y * nw_ref[...] * _silu(gg_ref[lo:hi, cols])).astype(o_ref.dtype)


def gdn_heads(proj, gab_t, conv_w, a_log, dt_bias, norm_w, tb):
    t = proj.shape[0]
    nh = N_REC_HEADS
    gw = nh * HEAD_W
    n_chunks = tb // CHUNK

    def group(block):
        return pl.BlockSpec((tb, gw), lambda i: (i, block))

    def whole(a):
        return pl.BlockSpec(a.shape, lambda i: (0,) * a.ndim)

    a_log = a_log.reshape(nh, 1)
    dt_bias = dt_bias.reshape(nh, 1)
    norm_w = norm_w.reshape(1, HEAD_W)
    return pl.pallas_call(
        functools.partial(_gdn_kernel, n_chunks=n_chunks),
        grid=(t // tb,),
        in_specs=[whole(a_log), whole(dt_bias), group(4), group(5), group(6), group(7),
                  pl.BlockSpec((2 * nh, tb), lambda i: (0, i)), whole(conv_w), whole(norm_w)],
        out_specs=pl.BlockSpec((tb, gw), lambda i: (i, 0)),
        out_shape=jax.ShapeDtypeStruct((t, gw), BF16),
        scratch_shapes=[pltpu.VMEM((nh, HEAD_W, HEAD_W), F32),
                        pltpu.VMEM((3, 8, gw), F32),
                        pltpu.VMEM((nh, tb, HEAD_W), F32),
                        pltpu.VMEM((nh, tb, HEAD_W), F32),
                        pltpu.VMEM((nh, tb, HEAD_W), F32),
                        pltpu.VMEM((nh, tb), F32),
                        pltpu.VMEM((nh, tb), F32),
                        pltpu.VMEM((nh, tb, HEAD_W), F32),
                        pltpu.VMEM((nh, n_chunks, 2 * CHUNK, HEAD_W), BF16),
                        pltpu.VMEM((nh, n_chunks, HEAD_W, CHUNK), BF16),
                        pltpu.VMEM((nh, n_chunks, CHUNK, CHUNK), BF16),
                        pltpu.VMEM((nh, n_chunks, 8, HEAD_W), F32)],
        compiler_params=_cparams(("arbitrary",)),
        name="gdn_heads",
    )(a_log, dt_bias, proj, proj, proj, proj, gab_t, conv_w, norm_w)


def _sb_kernel(q_ref, k_ref, v_ref, o_ref, acc_ref, c_ref, *, n_sub):
    blk = SB_BLK
    qb0 = pl.program_id(1) * n_sub
    row = lax.broadcasted_iota(jnp.int32, (blk, blk), 0)
    col = lax.broadcasted_iota(jnp.int32, (blk, blk), 1)
    later = (row > col).astype(BF16)
    causal = col < row

    def tiles(subs, s, masked):
        n = range(len(subs))
        start = [pl.multiple_of((qb0 + i - s) * blk, blk) for i in subs]
        z = [_dot_nt(q_ref[i * blk:(i + 1) * blk, :], k_ref[pl.ds(start[x], blk), :])
             for x, i in enumerate(subs)]
        sp = [_softplus(z[x]) for x in n]
        log_1m = [jnp.where(causal, -sp[x], 0.0) if masked else -sp[x] for x in n]
        after = [_dot(log_1m[x].astype(BF16), later) for x in n]
        c = [c_ref[i] for i in subs]
        a = [jnp.exp((z[x] - sp[x]) + after[x] + c[x]) for x in n]
        if masked:
            a = [jnp.where(causal, a[x], 0.0) for x in n]
        c_max = []
        for x, i in enumerate(subs):
            acc_ref[i] += _dot(a[x].astype(BF16), v_ref[pl.ds(start[x], blk), :])
            c_new = c[x] + after[x][:, :1] + log_1m[x][:, :1]
            c_ref[i] = c_new
            c_max.append(jnp.max(c_new))
        return c_max

    all_subs = list(range(n_sub))
    acc_ref[...] = jnp.zeros_like(acc_ref)
    c_ref[...] = jnp.zeros_like(c_ref)
    c_max = tiles(all_subs, 0, True)

    def cond(carry):
        return jnp.logical_and(carry[0] <= qb0,
                               functools.reduce(jnp.maximum, carry[1:]) > EXP_ZERO_BELOW)

    def body(carry):
        return (carry[0] + 1,) + tuple(tiles(all_subs, carry[0], False))

    carry = lax.while_loop(cond, body, (jnp.int32(1),) + tuple(c_max))
    for i in range(1, n_sub):
        lax.while_loop(
            lambda c, i=i: jnp.logical_and(c[0] <= qb0 + i, c[1] > EXP_ZERO_BELOW),
            lambda c, i=i: (c[0] + 1, tiles([i], c[0], False)[0]),
            (carry[0], carry[1 + i]))
    for i in range(n_sub):
        o_ref[i * blk:(i + 1) * blk, :] = acc_ref[i].astype(o_ref.dtype)


def sb_attention(qkv, n_heads, n_sub):
    t = qkv.shape[0]
    qb = n_sub * SB_BLK
    return pl.pallas_call(
        functools.partial(_sb_kernel, n_sub=n_sub),
        grid=(n_heads, t // qb),
        in_specs=[pl.BlockSpec((qb, HEAD_W), lambda h, i: (i, h)),
                  pl.BlockSpec((t, HEAD_W), lambda h, i: (0, n_heads + h)),
                  pl.BlockSpec((t, HEAD_W), lambda h, i: (0, 2 * n_heads + h))],
        out_specs=pl.BlockSpec((qb, HEAD_W), lambda h, i: (i, h)),
        out_shape=jax.ShapeDtypeStruct((t, n_heads * HEAD_W), BF16),
        scratch_shapes=[pltpu.VMEM((n_sub, SB_BLK, HEAD_W), F32),
                        pltpu.VMEM((n_sub, SB_BLK, 1), F32)],
        compiler_params=_cparams(("parallel", "arbitrary")),
        name="sb_attention",
    )(qkv, qkv, qkv)


def _pick(n, candidates):
    for c in candidates:
        if n % c == 0:
            return c
    return n


def kernel(x, mix_norm, a_w_in, a_conv_w, a_a_log, a_dt_bias, a_lb_logits, a_hgrn_norm,
           a_gdn_norm, a_w_out, c_w_qkv, c_w_o, mlp_norm, mlp_w1, mlp_w2, final_norm):
    bsz, t, d = x.shape
    assert bsz == 1
    depth = mix_norm.shape[0]
    nh = N_REC_HEADS
    kw = nh * HEAD_W
    xs = x.reshape(t, d)
    tm = _pick(t, (512, 256, 128, 64))
    tb = _pick(t, (256, 128, 64))
    sb_heads = d // HEAD_W
    sb_sub = _pick(t // SB_BLK, (8, 4, 2, 1))

    lb_all = jnp.cumsum(jax.nn.softmax(a_lb_logits.astype(F32), axis=0), axis=0)
    for layer in range(depth):
        j = layer // 2
        if layer % 2 == 0:
            w_in = a_w_in[j]
            n_gate = 2 * nh
            w_main = jnp.concatenate([w_in[:, :7 * kw], w_in[:, 7 * kw + n_gate:]], axis=1).astype(BF16)
            w_gate = jnp.pad(w_in[:, 7 * kw:7 * kw + n_gate], ((0, 0), (0, HEAD_W - n_gate))).astype(BF16)
            proj = norm_matmul(xs, mix_norm[layer], w_main, jnp.ones((8 * kw,), F32), F32,
                               tm, _pick(8 * kw, (1024,)))
            gates = norm_matmul(xs, mix_norm[layer], w_gate, jnp.ones((HEAD_W,), F32), F32,
                                tm, HEAD_W)
            gab_t = gates[:, :n_gate].T
            o_a = hgrn_heads(proj, lb_all[j], a_hgrn_norm[j], tb)
            o_b = gdn_heads(proj, gab_t, a_conv_w[j], a_a_log[j], a_dt_bias[j], a_gdn_norm[j], tb)
            w_out = a_w_out[j].astype(BF16)
            xs = proj_residual([o_a, o_b], [w_out[:kw], w_out[kw:]], xs, tm, _pick(d, (1024,)))
        else:
            scale = jnp.concatenate([jnp.full((d,), HEAD_W ** -0.5, F32), jnp.ones((2 * d,), F32)])
            qkv = norm_matmul(xs, mix_norm[layer], c_w_qkv[j].astype(BF16), scale, BF16,
                              tm, _pick(3 * d, (1024,)))
            o_c = sb_attention(qkv, sb_heads, sb_sub)
            xs = proj_residual([o_c], [c_w_o[j].astype(BF16)], xs, tm, _pick(d, (1024,)))
        last = layer == depth - 1
        xs = mlp_residual(xs, mlp_norm[layer], mlp_w1[layer].astype(BF16), mlp_w2[layer].astype(BF16),
                          final_norm, last, tm, _pick(mlp_w1.shape[2], (512,)))
    return xs.reshape(bsz, t, d)
```

```python
import functools

import jax
import jax.numpy as jnp
from jax import lax
from jax.experimental import pallas as pl
from jax.experimental.pallas import tpu as pltpu

F32 = jnp.float32
BF16 = jnp.bfloat16
EPS = 1e-6
HEAD_W = 128
CHUNK = 64
SUB = 16
CONV_W = 4
SB_BLK = 128
N_REC_HEADS = 8
WY_GROUP = 4
VMEM_LIMIT = 48 * 1024 * 1024
HI = lax.Precision.HIGHEST
EXP_ZERO_BELOW = -104.0


def _cparams(sem):
    return pltpu.CompilerParams(dimension_semantics=sem, vmem_limit_bytes=VMEM_LIMIT)


def _dot(a, b, precision=None):
    return jnp.dot(a, b, preferred_element_type=F32, precision=precision)


def _dot_nt(a, b, precision=None):
    return lax.dot_general(a, b, (((1,), (1,)), ((), ())),
                           preferred_element_type=F32, precision=precision)


def _rms(x, gain):
    return x * lax.rsqrt(jnp.mean(x * x, axis=-1, keepdims=True) + EPS) * gain


def _sigmoid(x):
    return 0.5 * jnp.tanh(0.5 * x) + 0.5


def _silu(x):
    return x * _sigmoid(x)


def _softplus(x):
    return jnp.maximum(x, 0.0) + jnp.log(1.0 + jnp.exp(-jnp.abs(x)))


def _norm_matmul_kernel(*refs, has_extra):
    if has_extra:
        x_ref, g_ref, w_ref, cs_ref, we_ref, o_ref, oe_ref, h_ref = refs
    else:
        x_ref, g_ref, w_ref, cs_ref, o_ref, h_ref = refs

    @pl.when(pl.program_id(1) == 0)
    def _():
        h_ref[...] = _rms(x_ref[...], g_ref[...]).astype(BF16)
        if has_extra:
            oe_ref[...] = _dot(h_ref[...], we_ref[...])

    acc = _dot(h_ref[...], w_ref[...]) * cs_ref[...]
    for c in range(o_ref.shape[0]):
        o_ref[c] = acc[:, c * HEAD_W:(c + 1) * HEAD_W].astype(o_ref.dtype)


def norm_matmul(x, gain, w, col_scale, out_dtype, tm, tn, w_extra=None):
    m, k = x.shape
    n = w.shape[1]
    has_extra = w_extra is not None
    in_specs = [
        pl.BlockSpec((tm, k), lambda i, j: (i, 0)),
        pl.BlockSpec((1, k), lambda i, j: (0, 0)),
        pl.BlockSpec((k, tn), lambda i, j: (0, j)),
        pl.BlockSpec((1, tn), lambda i, j: (0, j)),
    ]
    out_specs = [pl.BlockSpec((tn // HEAD_W, tm, HEAD_W), lambda i, j: (j, i, 0))]
    out_shape = [jax.ShapeDtypeStruct((n // HEAD_W, m, HEAD_W), out_dtype)]
    args = [x, gain.reshape(1, k), w, col_scale.reshape(1, n)]
    if has_extra:
        in_specs.append(pl.BlockSpec((k, HEAD_W), lambda i, j: (0, 0)))
        out_specs.append(pl.BlockSpec((tm, HEAD_W), lambda i, j: (i, 0)))
        out_shape.append(jax.ShapeDtypeStruct((m, HEAD_W), F32))
        args.append(w_extra)
    return pl.pallas_call(
        functools.partial(_norm_matmul_kernel, has_extra=has_extra),
        grid=(m // tm, n // tn),
        in_specs=in_specs,
        out_specs=out_specs,
        out_shape=out_shape,
        scratch_shapes=[pltpu.VMEM((tm, k), BF16)],
        compiler_params=_cparams(("parallel", "arbitrary")),
        name="norm_matmul",
    )(*args)


def _proj_residual_kernel(*refs, n_lhs):
    lhs = refs[:n_lhs]
    ws = refs[n_lhs:2 * n_lhs]
    x_ref, o_ref = refs[2 * n_lhs], refs[2 * n_lhs + 1]
    acc = x_ref[...]
    for a_ref, w_ref in zip(lhs, ws):
        acc = acc + _dot(a_ref[...], w_ref[...])
    o_ref[...] = acc


def proj_residual(lhs_list, w_list, x, tm, tn):
    m, n = x.shape
    n_lhs = len(lhs_list)
    in_specs = [pl.BlockSpec((tm, a.shape[1]), lambda i, j: (i, 0)) for a in lhs_list]
    in_specs += [pl.BlockSpec((w.shape[0], tn), lambda i, j: (0, j)) for w in w_list]
    in_specs += [pl.BlockSpec((tm, tn), lambda i, j: (i, j))]
    return pl.pallas_call(
        functools.partial(_proj_residual_kernel, n_lhs=n_lhs),
        grid=(m // tm, n // tn),
        in_specs=in_specs,
        out_specs=pl.BlockSpec((tm, tn), lambda i, j: (i, j)),
        out_shape=jax.ShapeDtypeStruct((m, n), F32),
        compiler_params=_cparams(("parallel", "arbitrary")),
        name="proj_residual",
    )(*lhs_list, *w_list, x)


def _mlp_kernel(x_ref, g_ref, w1_ref, w2_ref, fg_ref, o_ref, h_ref, *, final_norm):
    j = pl.program_id(1)

    @pl.when(j == 0)
    def _():
        x = x_ref[...]
        h_ref[...] = _rms(x, g_ref[...]).astype(BF16)
        o_ref[...] = x

    a = _dot(h_ref[...], w1_ref[...])
    a = jnp.square(jnp.maximum(a, 0.0)).astype(BF16)
    o_ref[...] += _dot(a, w2_ref[...])

    if final_norm:
        @pl.when(j == pl.num_programs(1) - 1)
        def _():
            o_ref[...] = _rms(o_ref[...], fg_ref[...])


def mlp_residual(x, gain, w1, w2, final_gain, final_norm, tm, tf):
    m, d = x.shape
    ff = w1.shape[1]
    return pl.pallas_call(
        functools.partial(_mlp_kernel, final_norm=final_norm),
        grid=(m // tm, ff // tf),
        in_specs=[
            pl.BlockSpec((tm, d), lambda i, j: (i, 0)),
            pl.BlockSpec((1, d), lambda i, j: (0, 0)),
            pl.BlockSpec((d, tf), lambda i, j: (0, j)),
            pl.BlockSpec((tf, d), lambda i, j: (j, 0)),
            pl.BlockSpec((1, d), lambda i, j: (0, 0)),
        ],
        out_specs=pl.BlockSpec((tm, d), lambda i, j: (i, 0)),
        out_shape=jax.ShapeDtypeStruct((m, d), F32),
        scratch_shapes=[pltpu.VMEM((tm, d), BF16)],
        compiler_params=_cparams(("parallel", "arbitrary")),
        name="mlp_residual",
    )(x, gain.reshape(1, d), w1, w2, final_gain.reshape(1, d))


def _hgrn_kernel(hq_ref, hf_ref, hi_ref, hg_ref, lb_ref, nw_ref, o_ref, st_ref, *, n_chunks):
    @pl.when(pl.program_id(1) == 0)
    def _():
        st_ref[...] = jnp.zeros_like(st_ref)

    chunks = range(n_chunks)
    n_sub = CHUNK // SUB
    lb = lb_ref[...]
    row = lax.broadcasted_iota(jnp.int32, (CHUNK, CHUNK), 0)
    col = lax.broadcasted_iota(jnp.int32, (CHUNK, CHUNK), 1)
    tril = (col <= row).astype(F32)

    hf = hf_ref[...]
    log_f = jnp.log(lb + (1.0 - lb) * _sigmoid(hf))
    k_all = (1.0 - lb) * _sigmoid(-hf)
    q_all = _silu(hq_ref[...])
    vb_all = hi_ref[...].astype(BF16)

    rows = [slice(c * CHUNK, (c + 1) * CHUNK) for c in chunks]
    q = [q_all[r] for r in rows]
    k = [k_all[r] for r in rows]
    vb = [vb_all[r] for r in rows]
    b = [_dot(tril, log_f[r], precision=HI) for r in rows]
    o_intra = [[] for _ in chunks]
    for i in range(n_sub):
        lo, hi = i * SUB, (i + 1) * SUB
        srow = lax.broadcasted_iota(jnp.int32, (SUB, hi), 0)
        scol = lax.broadcasted_iota(jnp.int32, (SUB, hi), 1)
        causal = scol <= srow + lo
        b_ref = [b[c][lo:lo + 1, :] for c in chunks]
        qs = [(q[c][lo:hi] * jnp.exp(b[c][lo:hi] - b_ref[c])).astype(BF16) for c in chunks]
        ks = [(k[c][:hi] * jnp.exp(b_ref[c] - b[c][:hi])).astype(BF16) for c in chunks]
        att = [jnp.where(causal, _dot_nt(qs[c], ks[c]), 0.0).astype(BF16) for c in chunks]
        for c in chunks:
            o_intra[c].append(_dot(att[c], vb[c][:hi]))
    b_last = [b[c][CHUNK - 1:CHUNK, :] for c in chunks]
    qd = [(q[c] * jnp.exp(b[c])).astype(BF16) for c in chunks]
    kd = [(k[c] * jnp.exp(b_last[c] - b[c])).astype(BF16) for c in chunks]
    p = [_dot(vb[c].T, kd[c]) for c in chunks]
    d = [jnp.exp(b_last[c]) for c in chunks]
    gate = nw_ref[...] * _silu(hg_ref[...])

    st = st_ref[...]
    for c in chunks:
        o = jnp.concatenate(o_intra[c], axis=0) + _dot_nt(qd[c], st.astype(BF16))
        st = st * d[c] + p[c]
        y = o * lax.rsqrt(jnp.mean(o * o, axis=-1, keepdims=True) + EPS)
        o_ref[rows[c], :] = (y * gate[rows[c]]).astype(o_ref.dtype)
    st_ref[...] = st


def hgrn_heads(proj, lb, norm_w, tb):
    t = proj.shape[1]
    nh = N_REC_HEADS

    def col(block0):
        return pl.BlockSpec((None, tb, HEAD_W), lambda h, i: (block0 + h, i, 0))

    return pl.pallas_call(
        functools.partial(_hgrn_kernel, n_chunks=tb // CHUNK),
        grid=(nh, t // tb),
        in_specs=[col(0), col(nh), col(2 * nh), col(3 * nh),
                  pl.BlockSpec((None, 1, HEAD_W), lambda h, i: (h, 0, 0)),
                  pl.BlockSpec((1, HEAD_W), lambda h, i: (0, 0))],
        out_specs=pl.BlockSpec((tb, HEAD_W), lambda h, i: (i, h)),
        out_shape=jax.ShapeDtypeStruct((t, nh * HEAD_W), BF16),
        scratch_shapes=[pltpu.VMEM((HEAD_W, HEAD_W), F32)],
        compiler_params=_cparams(("parallel", "arbitrary")),
        name="hgrn_heads",
    )(proj, proj, proj, proj, lb.reshape(nh, 1, HEAD_W), norm_w.reshape(1, HEAD_W))


def _gdn_kernel(alog_ref, dtb_ref, gq_ref, gk_ref, gv_ref, gg_ref, gab_ref, cw_ref, nw_ref, o_ref,
                s_ref, tail_ref, q_s, k_s, v_s, g_s, beta_s, u_s, wq_s, kdt_s, qk_s, gl_s, *, n_chunks):
    tb = n_chunks * CHUNK
    nh = N_REC_HEADS

    @pl.when(pl.program_id(0) == 0)
    def _():
        s_ref[...] = jnp.zeros_like(s_ref)
        tail_ref[...] = jnp.zeros_like(tail_ref)

    def conv_silu(u_ref, slot, h):
        cols = slice(h * HEAD_W, (h + 1) * HEAD_W)
        u = u_ref[h]
        ext = jnp.concatenate([tail_ref[slot, :, cols], u], axis=0)
        w = cw_ref[:, slot * nh * HEAD_W + h * HEAD_W:slot * nh * HEAD_W + (h + 1) * HEAD_W]
        acc = u * w[CONV_W - 1:CONV_W, :]
        for j in range(CONV_W - 1):
            acc = acc + pltpu.roll(ext, CONV_W - 1 - j, axis=0)[8:, :] * w[j:j + 1, :]
        tail_ref[slot, :, cols] = u[tb - 8:, :]
        return _silu(acc)

    for h in range(nh):
        qh = conv_silu(gq_ref, 0, h)
        kh = conv_silu(gk_ref, 1, h)
        q_s[h] = qh * lax.rsqrt(jnp.sum(qh * qh, axis=-1, keepdims=True) + EPS) * (HEAD_W ** -0.5)
        k_s[h] = kh * lax.rsqrt(jnp.sum(kh * kh, axis=-1, keepdims=True) + EPS)
        v_s[h] = conv_silu(gv_ref, 2, h)
    g = -jnp.exp(alog_ref[...]) * _softplus(gab_ref[:nh, :] + dtb_ref[...])
    brow = lax.broadcasted_iota(jnp.int32, (tb, tb), 0)
    bcol = lax.broadcasted_iota(jnp.int32, (tb, tb), 1)
    same_chunk_upto = jnp.logical_and(brow <= bcol, brow // CHUNK == bcol // CHUNK).astype(F32)
    g_s[...] = _dot(g, same_chunk_upto, precision=HI)
    beta_s[...] = _sigmoid(gab_ref[nh:, :])

    row = lax.broadcasted_iota(jnp.int32, (CHUNK, CHUNK), 0)
    col = lax.broadcasted_iota(jnp.int32, (CHUNK, CHUNK), 1)
    tril = col <= row
    strict = col < row
    eye = col == row
    eye_f = eye.astype(F32)

    def wy_heads(hg, carry):
        items = [(hg * WY_GROUP + dh, c) for dh in range(WY_GROUP) for c in range(n_chunks)]
        n = range(len(items))
        rows = [slice(c * CHUNK, (c + 1) * CHUNK) for _, c in items]
        q = [q_s[h, rows[x], :] for x, (h, _) in enumerate(items)]
        k = [k_s[h, rows[x], :] for x, (h, _) in enumerate(items)]
        gc_all = [g_s[pl.ds(hg * WY_GROUP + dh, 1), :] for dh in range(WY_GROUP)]
        beta_all = [beta_s[pl.ds(hg * WY_GROUP + dh, 1), :] for dh in range(WY_GROUP)]
        gc_row = [jnp.broadcast_to(gc_all[x // n_chunks][:, rows[x]], (CHUNK, CHUNK)) for x in n]
        beta_b = [jnp.broadcast_to(beta_all[x // n_chunks][:, rows[x]], (CHUNK, CHUNK)) for x in n]
        gc_col = [jnp.sum(jnp.where(eye, gc_row[x], 0.0), axis=-1, keepdims=True) for x in n]
        beta_col = [jnp.sum(jnp.where(eye, beta_b[x], 0.0), axis=-1, keepdims=True) for x in n]
        decay = [jnp.where(tril, jnp.exp(jnp.minimum(gc_col[x] - gc_row[x], 0.0)), 0.0) for x in n]
        kb = [k[x] * beta_col[x] for x in n]
        prod = [_dot_nt(jnp.concatenate([kb[x], q[x]], axis=0).astype(BF16), k[x].astype(BF16))
                for x in n]
        n1 = [jnp.where(strict, -prod[x][:CHUNK] * decay[x], 0.0) for x in n]
        inv = [eye_f + n1[x] for x in n]
        npow = [n1[x].astype(BF16) for x in n]
        for _ in range(5):
            npow = [_dot(npow[x], npow[x]).astype(BF16) for x in n]
            inv = [inv[x] + _dot(inv[x].astype(BF16), npow[x]) for x in n]
        e_gc = [jnp.exp(gc_col[x]) for x in n]
        rhs = [jnp.concatenate([v_s[h, rows[x], :] * beta_col[x], kb[x] * e_gc[x]], axis=1)
               for x, (h, _) in enumerate(items)]
        sol = [_dot(inv[x].astype(BF16), rhs[x].astype(BF16)) for x in n]
        for x, (h, c) in enumerate(items):
            g_last = gc_col[x][CHUNK - 1:CHUNK, :]
            u_s[h, rows[x], :] = sol[x][:, :HEAD_W]
            wq_s[h, c, :CHUNK, :] = sol[x][:, HEAD_W:].astype(BF16)
            wq_s[h, c, CHUNK:, :] = (q[x] * e_gc[x]).astype(BF16)
            kdt_s[h, c] = (k[x] * jnp.exp(g_last - gc_col[x])).T.astype(BF16)
            qk_s[h, c] = (prod[x][CHUNK:] * decay[x]).astype(BF16)
            gl_s[h, c] = jnp.broadcast_to(jnp.exp(g_last), (8, HEAD_W))
        return carry

    lax.fori_loop(0, nh // WY_GROUP, wy_heads, 0)

    for c in range(n_chunks):
        lo, hi = c * CHUNK, (c + 1) * CHUNK
        for h in range(nh):
            cols = slice(h * HEAD_W, (h + 1) * HEAD_W)
            s = s_ref[h]
            ws = _dot(wq_s[h, c], s.astype(BF16))
            v_new = (u_s[h, lo:hi, :] - ws[:CHUNK]).astype(BF16)
            o = ws[CHUNK:] + _dot(qk_s[h, c], v_new)
            s_ref[h] = s * gl_s[h, c][:1, :] + _dot(kdt_s[h, c], v_new)
            y = o * lax.rsqrt(jnp.mean(o * o, axis=-1, keepdims=True) + EPS)
            o_ref[lo:hi, cols] = (y * nw_ref[...] * _silu(gg_ref[h, lo:hi, :])).astype(o_ref.dtype)


def gdn_heads(proj, gab_t, conv_w, a_log, dt_bias, norm_w, tb):
    t = proj.shape[1]
    nh = N_REC_HEADS
    gw = nh * HEAD_W
    n_chunks = tb // CHUNK

    def group(block):
        return pl.BlockSpec((nh, tb, HEAD_W), lambda i: (block, i, 0))

    def whole(a):
        return pl.BlockSpec(a.shape, lambda i: (0,) * a.ndim)

    a_log = a_log.reshape(nh, 1)
    dt_bias = dt_bias.reshape(nh, 1)
    norm_w = norm_w.reshape(1, HEAD_W)
    return pl.pallas_call(
        functools.partial(_gdn_kernel, n_chunks=n_chunks),
        grid=(t // tb,),
        in_specs=[whole(a_log), whole(dt_bias), group(4), group(5), group(6), group(7),
                  pl.BlockSpec((2 * nh, tb), lambda i: (0, i)), whole(conv_w), whole(norm_w)],
        out_specs=pl.BlockSpec((tb, gw), lambda i: (i, 0)),
        out_shape=jax.ShapeDtypeStruct((t, gw), BF16),
        scratch_shapes=[pltpu.VMEM((nh, HEAD_W, HEAD_W), F32),
                        pltpu.VMEM((3, 8, gw), F32),
                        pltpu.VMEM((nh, tb, HEAD_W), F32),
                        pltpu.VMEM((nh, tb, HEAD_W), F32),
                        pltpu.VMEM((nh, tb, HEAD_W), F32),
                        pltpu.VMEM((nh, tb), F32),
                        pltpu.VMEM((nh, tb), F32),
                        pltpu.VMEM((nh, tb, HEAD_W), F32),
                        pltpu.VMEM((nh, n_chunks, 2 * CHUNK, HEAD_W), BF16),
                        pltpu.VMEM((nh, n_chunks, HEAD_W, CHUNK), BF16),
                        pltpu.VMEM((nh, n_chunks, CHUNK, CHUNK), BF16),
                        pltpu.VMEM((nh, n_chunks, 8, HEAD_W), F32)],
        compiler_params=_cparams(("arbitrary",)),
        name="gdn_heads",
    )(a_log, dt_bias, proj, proj, proj, proj, gab_t, conv_w, norm_w)


def _sb_kernel(q_ref, k_ref, v_ref, o_ref, acc_ref, c_ref, *, n_sub):
    blk = SB_BLK
    qb0 = pl.program_id(1) * n_sub
    row = lax.broadcasted_iota(jnp.int32, (blk, blk), 0)
    col = lax.broadcasted_iota(jnp.int32, (blk, blk), 1)
    later = (row > col).astype(BF16)
    causal = col < row

    def tiles(subs, s, masked):
        n = range(len(subs))
        start = [pl.multiple_of((qb0 + i - s) * blk, blk) for i in subs]
        z = [_dot_nt(q_ref[i * blk:(i + 1) * blk, :], k_ref[pl.ds(start[x], blk), :])
             for x, i in enumerate(subs)]
        sp = [_softplus(z[x]) for x in n]
        log_1m = [jnp.where(causal, -sp[x], 0.0) if masked else -sp[x] for x in n]
        after = [_dot(log_1m[x].astype(BF16), later) for x in n]
        c = [c_ref[i] for i in subs]
        a = [jnp.exp((z[x] - sp[x]) + after[x] + c[x]) for x in n]
        if masked:
            a = [jnp.where(causal, a[x], 0.0) for x in n]
        c_max = []
        for x, i in enumerate(subs):
            acc_ref[i] += _dot(a[x].astype(BF16), v_ref[pl.ds(start[x], blk), :])
            c_new = c[x] + after[x][:, :1] + log_1m[x][:, :1]
            c_ref[i] = c_new
            c_max.append(jnp.max(c_new))
        return c_max

    all_subs = list(range(n_sub))
    acc_ref[...] = jnp.zeros_like(acc_ref)
    c_ref[...] = jnp.zeros_like(c_ref)
    c_max = tiles(all_subs, 0, True)

    def cond(carry):
        return jnp.logical_and(carry[0] <= qb0,
                               functools.reduce(jnp.maximum, carry[1:]) > EXP_ZERO_BELOW)

    def body(carry):
        return (carry[0] + 1,) + tuple(tiles(all_subs, carry[0], False))

    carry = lax.while_loop(cond, body, (jnp.int32(1),) + tuple(c_max))
    for i in range(1, n_sub):
        lax.while_loop(
            lambda c, i=i: jnp.logical_and(c[0] <= qb0 + i, c[1] > EXP_ZERO_BELOW),
            lambda c, i=i: (c[0] + 1, tiles([i], c[0], False)[0]),
            (carry[0], carry[1 + i]))
    for i in range(n_sub):
        o_ref[i * blk:(i + 1) * blk, :] = acc_ref[i].astype(o_ref.dtype)


def sb_attention(qkv, n_heads, n_sub):
    t = qkv.shape[1]
    qb = n_sub * SB_BLK
    return pl.pallas_call(
        functools.partial(_sb_kernel, n_sub=n_sub),
        grid=(n_heads, t // qb),
        in_specs=[pl.BlockSpec((None, qb, HEAD_W), lambda h, i: (h, i, 0)),
                  pl.BlockSpec((None, t, HEAD_W), lambda h, i: (n_heads + h, 0, 0)),
                  pl.BlockSpec((None, t, HEAD_W), lambda h, i: (2 * n_heads + h, 0, 0))],
        out_specs=pl.BlockSpec((qb, HEAD_W), lambda h, i: (i, h)),
        out_shape=jax.ShapeDtypeStruct((t, n_heads * HEAD_W), BF16),
        scratch_shapes=[pltpu.VMEM((n_sub, SB_BLK, HEAD_W), F32),
                        pltpu.VMEM((n_sub, SB_BLK, 1), F32)],
        compiler_params=_cparams(("parallel", "arbitrary")),
        name="sb_attention",
    )(qkv, qkv, qkv)


def _pick(n, candidates):
    for c in candidates:
        if n % c == 0:
            return c
    return n


def kernel(x, mix_norm, a_w_in, a_conv_w, a_a_log, a_dt_bias, a_lb_logits, a_hgrn_norm,
           a_gdn_norm, a_w_out, c_w_qkv, c_w_o, mlp_norm, mlp_w1, mlp_w2, final_norm):
    bsz, t, d = x.shape
    assert bsz == 1
    depth = mix_norm.shape[0]
    nh = N_REC_HEADS
    kw = nh * HEAD_W
    xs = x.reshape(t, d)
    tm = _pick(t, (1024, 512, 256, 128, 64))
    tb_hgrn = _pick(t, (1024, 512, 256, 128, 64))
    tb_gdn = _pick(t, (256, 128, 64))
    sb_heads = d // HEAD_W
    sb_sub = _pick(t // SB_BLK, (8, 4, 2, 1))

    lb_all = jnp.cumsum(jax.nn.softmax(a_lb_logits.astype(F32), axis=0), axis=0)
    for layer in range(depth):
        j = layer // 2
        if layer % 2 == 0:
            w_in = a_w_in[j]
            n_gate = 2 * nh
            w_main = jnp.concatenate([w_in[:, :7 * kw], w_in[:, 7 * kw + n_gate:]], axis=1).astype(BF16)
            w_gate = jnp.pad(w_in[:, 7 * kw:7 * kw + n_gate], ((0, 0), (0, HEAD_W - n_gate))).astype(BF16)
            proj, gates = norm_matmul(xs, mix_norm[layer], w_main, jnp.ones((8 * kw,), F32), F32,
                                      tm, _pick(8 * kw, (1024,)), w_extra=w_gate)
            gab_t = gates[:, :n_gate].T
            o_a = hgrn_heads(proj, lb_all[j], a_hgrn_norm[j], tb_hgrn)
            o_b = gdn_heads(proj, gab_t, a_conv_w[j], a_a_log[j], a_dt_bias[j], a_gdn_norm[j], tb_gdn)
            w_out = a_w_out[j].astype(BF16)
            xs = proj_residual([o_a, o_b], [w_out[:kw], w_out[kw:]], xs, tm, _pick(d, (1024,)))
        else:
            scale = jnp.concatenate([jnp.full((d,), HEAD_W ** -0.5, F32), jnp.ones((2 * d,), F32)])
            qkv, = norm_matmul(xs, mix_norm[layer], c_w_qkv[j].astype(BF16), scale, BF16,
                               tm, _pick(3 * d, (1024,)))
            o_c = sb_attention(qkv, sb_heads, sb_sub)
            xs = proj_residual([o_c], [c_w_o[j].astype(BF16)], xs, tm, _pick(d, (1024,)))
        last = layer == depth - 1
        xs = mlp_residual(xs, mlp_norm[layer], mlp_w1[layer].astype(BF16), mlp_w2[layer].astype(BF16),
                          final_norm, last, tm, _pick(mlp_w1.shape[2], (512,)))
    return xs.reshape(bsz, t, d)
```

```python
import functools

import jax
import jax.numpy as jnp
from jax import lax
from jax.experimental import pallas as pl
from jax.experimental.pallas import tpu as pltpu

F32 = jnp.float32
BF16 = jnp.bfloat16
EPS = 1e-6
HEAD_W = 128
CHUNK = 64
SUB = 16
CONV_W = 4
SB_BLK = 128
N_REC_HEADS = 8
WY_GROUP = 8
VMEM_LIMIT = 48 * 1024 * 1024
HI = lax.Precision.HIGHEST
EXP_ZERO_BELOW = -104.0


def _cparams(sem):
    return pltpu.CompilerParams(dimension_semantics=sem, vmem_limit_bytes=VMEM_LIMIT)


def _dot(a, b, precision=None):
    return jnp.dot(a, b, preferred_element_type=F32, precision=precision)


def _dot_nt(a, b, precision=None):
    return lax.dot_general(a, b, (((1,), (1,)), ((), ())),
                           preferred_element_type=F32, precision=precision)


def _rms(x, gain):
    return x * lax.rsqrt(jnp.mean(x * x, axis=-1, keepdims=True) + EPS) * gain


def _sigmoid(x):
    return 0.5 * jnp.tanh(0.5 * x) + 0.5


def _silu(x):
    return x * _sigmoid(x)


def _softplus(x):
    return jnp.maximum(x, 0.0) + jnp.log(1.0 + jnp.exp(-jnp.abs(x)))


def _norm_matmul_kernel(*refs, has_extra):
    if has_extra:
        x_ref, g_ref, w_ref, cs_ref, we_ref, o_ref, oe_ref, h_ref = refs
    else:
        x_ref, g_ref, w_ref, cs_ref, o_ref, h_ref = refs

    @pl.when(pl.program_id(1) == 0)
    def _():
        h_ref[...] = _rms(x_ref[...], g_ref[...]).astype(BF16)
        if has_extra:
            oe_ref[...] = _dot(h_ref[...], we_ref[...])

    acc = _dot(h_ref[...], w_ref[...]) * cs_ref[...]
    for c in range(o_ref.shape[0]):
        o_ref[c] = acc[:, c * HEAD_W:(c + 1) * HEAD_W].astype(o_ref.dtype)


def norm_matmul(x, gain, w, col_scale, out_dtype, tm, tn, w_extra=None):
    m, k = x.shape
    n = w.shape[1]
    has_extra = w_extra is not None
    in_specs = [
        pl.BlockSpec((tm, k), lambda i, j: (i, 0)),
        pl.BlockSpec((1, k), lambda i, j: (0, 0)),
        pl.BlockSpec((k, tn), lambda i, j: (0, j)),
        pl.BlockSpec((1, tn), lambda i, j: (0, j)),
    ]
    out_specs = [pl.BlockSpec((tn // HEAD_W, tm, HEAD_W), lambda i, j: (j, i, 0))]
    out_shape = [jax.ShapeDtypeStruct((n // HEAD_W, m, HEAD_W), out_dtype)]
    args = [x, gain.reshape(1, k), w, col_scale.reshape(1, n)]
    if has_extra:
        in_specs.append(pl.BlockSpec((k, HEAD_W), lambda i, j: (0, 0)))
        out_specs.append(pl.BlockSpec((tm, HEAD_W), lambda i, j: (i, 0)))
        out_shape.append(jax.ShapeDtypeStruct((m, HEAD_W), F32))
        args.append(w_extra)
    return pl.pallas_call(
        functools.partial(_norm_matmul_kernel, has_extra=has_extra),
        grid=(m // tm, n // tn),
        in_specs=in_specs,
        out_specs=out_specs,
        out_shape=out_shape,
        scratch_shapes=[pltpu.VMEM((tm, k), BF16)],
        compiler_params=_cparams(("parallel", "arbitrary")),
        name="norm_matmul",
    )(*args)


def _proj_residual_kernel(*refs, n_lhs):
    lhs = refs[:n_lhs]
    ws = refs[n_lhs:2 * n_lhs]
    x_ref, o_ref = refs[2 * n_lhs], refs[2 * n_lhs + 1]
    acc = x_ref[...]
    for a_ref, w_ref in zip(lhs, ws):
        acc = acc + _dot(a_ref[...], w_ref[...])
    o_ref[...] = acc


def proj_residual(lhs_list, w_list, x, tm, tn):
    m, n = x.shape
    n_lhs = len(lhs_list)
    in_specs = [pl.BlockSpec((tm, a.shape[1]), lambda i, j: (i, 0)) for a in lhs_list]
    in_specs += [pl.BlockSpec((w.shape[0], tn), lambda i, j: (0, j)) for w in w_list]
    in_specs += [pl.BlockSpec((tm, tn), lambda i, j: (i, j))]
    return pl.pallas_call(
        functools.partial(_proj_residual_kernel, n_lhs=n_lhs),
        grid=(m // tm, n // tn),
        in_specs=in_specs,
        out_specs=pl.BlockSpec((tm, tn), lambda i, j: (i, j)),
        out_shape=jax.ShapeDtypeStruct((m, n), F32),
        compiler_params=_cparams(("parallel", "arbitrary")),
        name="proj_residual",
    )(*lhs_list, *w_list, x)


def _mlp_kernel(x_ref, g_ref, w1_ref, w2_ref, fg_ref, o_ref, h_ref, *, final_norm):
    j = pl.program_id(1)

    @pl.when(j == 0)
    def _():
        x = x_ref[...]
        h_ref[...] = _rms(x, g_ref[...]).astype(BF16)
        o_ref[...] = x

    a = _dot(h_ref[...], w1_ref[...])
    a = jnp.square(jnp.maximum(a, 0.0)).astype(BF16)
    o_ref[...] += _dot(a, w2_ref[...])

    if final_norm:
        @pl.when(j == pl.num_programs(1) - 1)
        def _():
            o_ref[...] = _rms(o_ref[...], fg_ref[...])


def mlp_residual(x, gain, w1, w2, final_gain, final_norm, tm, tf):
    m, d = x.shape
    ff = w1.shape[1]
    return pl.pallas_call(
        functools.partial(_mlp_kernel, final_norm=final_norm),
        grid=(m // tm, ff // tf),
        in_specs=[
            pl.BlockSpec((tm, d), lambda i, j: (i, 0)),
            pl.BlockSpec((1, d), lambda i, j: (0, 0)),
            pl.BlockSpec((d, tf), lambda i, j: (0, j)),
            pl.BlockSpec((tf, d), lambda i, j: (j, 0)),
            pl.BlockSpec((1, d), lambda i, j: (0, 0)),
        ],
        out_specs=pl.BlockSpec((tm, d), lambda i, j: (i, 0)),
        out_shape=jax.ShapeDtypeStruct((m, d), F32),
        scratch_shapes=[pltpu.VMEM((tm, d), BF16)],
        compiler_params=_cparams(("parallel", "arbitrary")),
        name="mlp_residual",
    )(x, gain.reshape(1, d), w1, w2, final_gain.reshape(1, d))


def _hgrn_kernel(hq_ref, hf_ref, hi_ref, hg_ref, lb_ref, nw_ref, o_ref, st_ref, *, n_chunks):
    @pl.when(pl.program_id(1) == 0)
    def _():
        st_ref[...] = jnp.zeros_like(st_ref)

    chunks = range(n_chunks)
    n_sub = CHUNK // SUB
    lb = lb_ref[...]
    row = lax.broadcasted_iota(jnp.int32, (CHUNK, CHUNK), 0)
    col = lax.broadcasted_iota(jnp.int32, (CHUNK, CHUNK), 1)
    tril = (col <= row).astype(F32)

    hf = hf_ref[...]
    log_f = jnp.log(lb + (1.0 - lb) * _sigmoid(hf))
    k_all = (1.0 - lb) * _sigmoid(-hf)
    q_all = _silu(hq_ref[...])
    vb_all = hi_ref[...].astype(BF16)

    rows = [slice(c * CHUNK, (c + 1) * CHUNK) for c in chunks]
    q = [q_all[r] for r in rows]
    k = [k_all[r] for r in rows]
    vb = [vb_all[r] for r in rows]
    b = [_dot(tril, log_f[r], precision=HI) for r in rows]
    o_intra = [[] for _ in chunks]
    for i in range(n_sub):
        lo, hi = i * SUB, (i + 1) * SUB
        srow = lax.broadcasted_iota(jnp.int32, (SUB, hi), 0)
        scol = lax.broadcasted_iota(jnp.int32, (SUB, hi), 1)
        causal = scol <= srow + lo
        b_ref = [b[c][lo:lo + 1, :] for c in chunks]
        qs = [(q[c][lo:hi] * jnp.exp(b[c][lo:hi] - b_ref[c])).astype(BF16) for c in chunks]
        ks = [(k[c][:hi] * jnp.exp(b_ref[c] - b[c][:hi])).astype(BF16) for c in chunks]
        att = [jnp.where(causal, _dot_nt(qs[c], ks[c]), 0.0).astype(BF16) for c in chunks]
        for c in chunks:
            o_intra[c].append(_dot(att[c], vb[c][:hi]))
    b_last = [b[c][CHUNK - 1:CHUNK, :] for c in chunks]
    qd = [(q[c] * jnp.exp(b[c])).astype(BF16) for c in chunks]
    kd = [(k[c] * jnp.exp(b_last[c] - b[c])).astype(BF16) for c in chunks]
    p = [_dot(vb[c].T, kd[c]) for c in chunks]
    d = [jnp.exp(b_last[c]) for c in chunks]
    gate = nw_ref[...] * _silu(hg_ref[...])

    st = st_ref[...]
    for c in chunks:
        o = jnp.concatenate(o_intra[c], axis=0) + _dot_nt(qd[c], st.astype(BF16))
        st = st * d[c] + p[c]
        y = o * lax.rsqrt(jnp.mean(o * o, axis=-1, keepdims=True) + EPS)
        o_ref[rows[c], :] = (y * gate[rows[c]]).astype(o_ref.dtype)
    st_ref[...] = st


def hgrn_heads(proj, lb, norm_w, tb):
    t = proj.shape[1]
    nh = N_REC_HEADS

    def col(block0):
        return pl.BlockSpec((None, tb, HEAD_W), lambda h, i: (block0 + h, i, 0))

    return pl.pallas_call(
        functools.partial(_hgrn_kernel, n_chunks=tb // CHUNK),
        grid=(nh, t // tb),
        in_specs=[col(0), col(nh), col(2 * nh), col(3 * nh),
                  pl.BlockSpec((None, 1, HEAD_W), lambda h, i: (h, 0, 0)),
                  pl.BlockSpec((1, HEAD_W), lambda h, i: (0, 0))],
        out_specs=pl.BlockSpec((tb, HEAD_W), lambda h, i: (i, h)),
        out_shape=jax.ShapeDtypeStruct((t, nh * HEAD_W), BF16),
        scratch_shapes=[pltpu.VMEM((HEAD_W, HEAD_W), F32)],
        compiler_params=_cparams(("parallel", "arbitrary")),
        name="hgrn_heads",
    )(proj, proj, proj, proj, lb.reshape(nh, 1, HEAD_W), norm_w.reshape(1, HEAD_W))


def _gdn_kernel(alog_ref, dtb_ref, gq_ref, gk_ref, gv_ref, gg_ref, gab_ref, cw_ref, nw_ref, o_ref,
                s_ref, tail_ref, q_s, k_s, v_s, g_s, beta_s, u_s, wq_s, kdt_s, qk_s, gl_s, *, n_chunks):
    tb = n_chunks * CHUNK
    nh = N_REC_HEADS

    @pl.when(pl.program_id(0) == 0)
    def _():
        s_ref[...] = jnp.zeros_like(s_ref)
        tail_ref[...] = jnp.zeros_like(tail_ref)

    def conv_silu(u_ref, slot, h):
        cols = slice(h * HEAD_W, (h + 1) * HEAD_W)
        u = u_ref[h]
        ext = jnp.concatenate([tail_ref[slot, :, cols], u], axis=0)
        w = cw_ref[:, slot * nh * HEAD_W + h * HEAD_W:slot * nh * HEAD_W + (h + 1) * HEAD_W]
        acc = u * w[CONV_W - 1:CONV_W, :]
        for j in range(CONV_W - 1):
            acc = acc + pltpu.roll(ext, CONV_W - 1 - j, axis=0)[8:, :] * w[j:j + 1, :]
        tail_ref[slot, :, cols] = u[tb - 8:, :]
        return _silu(acc)

    for h in range(nh):
        qh = conv_silu(gq_ref, 0, h)
        kh = conv_silu(gk_ref, 1, h)
        q_s[h] = qh * lax.rsqrt(jnp.sum(qh * qh, axis=-1, keepdims=True) + EPS) * (HEAD_W ** -0.5)
        k_s[h] = kh * lax.rsqrt(jnp.sum(kh * kh, axis=-1, keepdims=True) + EPS)
        v_s[h] = conv_silu(gv_ref, 2, h)
    g = -jnp.exp(alog_ref[...]) * _softplus(gab_ref[:nh, :] + dtb_ref[...])
    brow = lax.broadcasted_iota(jnp.int32, (tb, tb), 0)
    bcol = lax.broadcasted_iota(jnp.int32, (tb, tb), 1)
    same_chunk_upto = jnp.logical_and(brow <= bcol, brow // CHUNK == bcol // CHUNK).astype(F32)
    g_s[...] = _dot(g, same_chunk_upto, precision=HI)
    beta_s[...] = _sigmoid(gab_ref[nh:, :])

    row = lax.broadcasted_iota(jnp.int32, (CHUNK, CHUNK), 0)
    col = lax.broadcasted_iota(jnp.int32, (CHUNK, CHUNK), 1)
    tril = col <= row
    strict = col < row
    eye = col == row
    eye_f = eye.astype(F32)

    def wy_heads(hg, carry):
        items = [(hg * WY_GROUP + dh, c) for dh in range(WY_GROUP) for c in range(n_chunks)]
        n = range(len(items))
        rows = [slice(c * CHUNK, (c + 1) * CHUNK) for _, c in items]
        q = [q_s[h, rows[x], :] for x, (h, _) in enumerate(items)]
        k = [k_s[h, rows[x], :] for x, (h, _) in enumerate(items)]
        gc_all = [g_s[pl.ds(hg * WY_GROUP + dh, 1), :] for dh in range(WY_GROUP)]
        beta_all = [beta_s[pl.ds(hg * WY_GROUP + dh, 1), :] for dh in range(WY_GROUP)]
        gc_row = [jnp.broadcast_to(gc_all[x // n_chunks][:, rows[x]], (CHUNK, CHUNK)) for x in n]
        beta_b = [jnp.broadcast_to(beta_all[x // n_chunks][:, rows[x]], (CHUNK, CHUNK)) for x in n]
        gc_col = [jnp.sum(jnp.where(eye, gc_row[x], 0.0), axis=-1, keepdims=True) for x in n]
        beta_col = [jnp.sum(jnp.where(eye, beta_b[x], 0.0), axis=-1, keepdims=True) for x in n]
        decay = [jnp.where(tril, jnp.exp(jnp.minimum(gc_col[x] - gc_row[x], 0.0)), 0.0) for x in n]
        kb = [k[x] * beta_col[x] for x in n]
        prod = [_dot_nt(jnp.concatenate([kb[x], q[x]], axis=0).astype(BF16), k[x].astype(BF16))
                for x in n]
        n1 = [jnp.where(strict, -prod[x][:CHUNK] * decay[x], 0.0) for x in n]
        inv = [eye_f + n1[x] for x in n]
        npow = [n1[x].astype(BF16) for x in n]
        for _ in range(5):
            npow = [_dot(npow[x], npow[x]).astype(BF16) for x in n]
            inv = [inv[x] + _dot(inv[x].astype(BF16), npow[x]) for x in n]
        e_gc = [jnp.exp(gc_col[x]) for x in n]
        rhs = [jnp.concatenate([v_s[h, rows[x], :] * beta_col[x], kb[x] * e_gc[x]], axis=1)
               for x, (h, _) in enumerate(items)]
        sol = [_dot(inv[x].astype(BF16), rhs[x].astype(BF16)) for x in n]
        for x, (h, c) in enumerate(items):
            g_last = gc_col[x][CHUNK - 1:CHUNK, :]
            u_s[h, rows[x], :] = sol[x][:, :HEAD_W]
            wq_s[h, c, :CHUNK, :] = sol[x][:, HEAD_W:].astype(BF16)
            wq_s[h, c, CHUNK:, :] = (q[x] * e_gc[x]).astype(BF16)
            kdt_s[h, c] = (k[x] * jnp.exp(g_last - gc_col[x])).T.astype(BF16)
            qk_s[h, c] = (prod[x][CHUNK:] * decay[x]).astype(BF16)
            gl_s[h, c] = jnp.broadcast_to(jnp.exp(g_last), (8, HEAD_W))
        return carry

    lax.fori_loop(0, nh // WY_GROUP, wy_heads, 0)

    heads = range(nh)
    s = [s_ref[h] for h in heads]
    for c in range(n_chunks):
        lo, hi = c * CHUNK, (c + 1) * CHUNK
        ws = [_dot(wq_s[h, c], s[h].astype(BF16)) for h in heads]
        v_new = [(u_s[h, lo:hi, :] - ws[h][:CHUNK]).astype(BF16) for h in heads]
        o = [ws[h][CHUNK:] + _dot(qk_s[h, c], v_new[h]) for h in heads]
        s = [s[h] * gl_s[h, c][:1, :] + _dot(kdt_s[h, c], v_new[h]) for h in heads]
        for h in heads:
            y = o[h] * lax.rsqrt(jnp.mean(o[h] * o[h], axis=-1, keepdims=True) + EPS)
            o_ref[lo:hi, h * HEAD_W:(h + 1) * HEAD_W] = (
                y * nw_ref[...] * _silu(gg_ref[h, lo:hi, :])).astype(o_ref.dtype)
    for h in heads:
        s_ref[h] = s[h]


def gdn_heads(proj, gab_t, conv_w, a_log, dt_bias, norm_w, tb):
    t = proj.shape[1]
    nh = N_REC_HEADS
    gw = nh * HEAD_W
    n_chunks = tb // CHUNK

    def group(block):
        return pl.BlockSpec((nh, tb, HEAD_W), lambda i: (block, i, 0))

    def whole(a):
        return pl.BlockSpec(a.shape, lambda i: (0,) * a.ndim)

    a_log = a_log.reshape(nh, 1)
    dt_bias = dt_bias.reshape(nh, 1)
    norm_w = norm_w.reshape(1, HEAD_W)
    return pl.pallas_call(
        functools.partial(_gdn_kernel, n_chunks=n_chunks),
        grid=(t // tb,),
        in_specs=[whole(a_log), whole(dt_bias), group(4), group(5), group(6), group(7),
                  pl.BlockSpec((2 * nh, tb), lambda i: (0, i)), whole(conv_w), whole(norm_w)],
        out_specs=pl.BlockSpec((tb, gw), lambda i: (i, 0)),
        out_shape=jax.ShapeDtypeStruct((t, gw), BF16),
        scratch_shapes=[pltpu.VMEM((nh, HEAD_W, HEAD_W), F32),
                        pltpu.VMEM((3, 8, gw), F32),
                        pltpu.VMEM((nh, tb, HEAD_W), F32),
                        pltpu.VMEM((nh, tb, HEAD_W), F32),
                        pltpu.VMEM((nh, tb, HEAD_W), F32),
                        pltpu.VMEM((nh, tb), F32),
                        pltpu.VMEM((nh, tb), F32),
                        pltpu.VMEM((nh, tb, HEAD_W), F32),
                        pltpu.VMEM((nh, n_chunks, 2 * CHUNK, HEAD_W), BF16),
                        pltpu.VMEM((nh, n_chunks, HEAD_W, CHUNK), BF16),
                        pltpu.VMEM((nh, n_chunks, CHUNK, CHUNK), BF16),
                        pltpu.VMEM((nh, n_chunks, 8, HEAD_W), F32)],
        compiler_params=_cparams(("arbitrary",)),
        name="gdn_heads",
    )(a_log, dt_bias, proj, proj, proj, proj, gab_t, conv_w, norm_w)


def _sb_kernel(q_ref, k_ref, v_ref, o_ref, acc_ref, c_ref, *, n_sub):
    blk = SB_BLK
    qb0 = pl.program_id(1) * n_sub
    row = lax.broadcasted_iota(jnp.int32, (blk, blk), 0)
    col = lax.broadcasted_iota(jnp.int32, (blk, blk), 1)
    later = (row > col).astype(BF16)
    causal = col < row

    def tiles(subs, s, masked):
        n = range(len(subs))
        start = [pl.multiple_of((qb0 + i - s) * blk, blk) for i in subs]
        z = [_dot_nt(q_ref[i * blk:(i + 1) * blk, :], k_ref[pl.ds(start[x], blk), :])
             for x, i in enumerate(subs)]
        sp = [_softplus(z[x]) for x in n]
        log_1m = [jnp.where(causal, -sp[x], 0.0) if masked else -sp[x] for x in n]
        after = [_dot(log_1m[x].astype(BF16), later) for x in n]
        c = [c_ref[i] for i in subs]
        a = [jnp.exp((z[x] - sp[x]) + after[x] + c[x]) for x in n]
        if masked:
            a = [jnp.where(causal, a[x], 0.0) for x in n]
        c_new = [c[x] + after[x][:, :1] + log_1m[x][:, :1] for x in n]
        for x, i in enumerate(subs):
            acc_ref[i] += _dot(a[x].astype(BF16), v_ref[pl.ds(start[x], blk), :])
            c_ref[i] = c_new[x]
        return jnp.max(functools.reduce(jnp.maximum, c_new))

    all_subs = list(range(n_sub))
    acc_ref[...] = jnp.zeros_like(acc_ref)
    c_ref[...] = jnp.zeros_like(c_ref)
    c_max = tiles(all_subs, 0, True)

    s_end, c_max = lax.while_loop(
        lambda carry: jnp.logical_and(carry[0] <= qb0, carry[1] > EXP_ZERO_BELOW),
        lambda carry: (carry[0] + 1, tiles(all_subs, carry[0], False)),
        (jnp.int32(1), c_max))

    @pl.when(c_max > EXP_ZERO_BELOW)
    def _():
        for i in range(1, n_sub):
            lax.while_loop(
                lambda carry, i=i: jnp.logical_and(carry[0] <= qb0 + i, carry[1] > EXP_ZERO_BELOW),
                lambda carry, i=i: (carry[0] + 1, tiles([i], carry[0], False)),
                (s_end, jnp.max(c_ref[i])))

    for i in range(n_sub):
        o_ref[i * blk:(i + 1) * blk, :] = acc_ref[i].astype(o_ref.dtype)


def sb_attention(qkv, n_heads, n_sub):
    t = qkv.shape[1]
    qb = n_sub * SB_BLK
    return pl.pallas_call(
        functools.partial(_sb_kernel, n_sub=n_sub),
        grid=(n_heads, t // qb),
        in_specs=[pl.BlockSpec((None, qb, HEAD_W), lambda h, i: (h, i, 0)),
                  pl.BlockSpec((None, t, HEAD_W), lambda h, i: (n_heads + h, 0, 0)),
                  pl.BlockSpec((None, t, HEAD_W), lambda h, i: (2 * n_heads + h, 0, 0))],
        out_specs=pl.BlockSpec((qb, HEAD_W), lambda h, i: (i, h)),
        out_shape=jax.ShapeDtypeStruct((t, n_heads * HEAD_W), BF16),
        scratch_shapes=[pltpu.VMEM((n_sub, SB_BLK, HEAD_W), F32),
                        pltpu.VMEM((n_sub, SB_BLK, 1), F32)],
        compiler_params=_cparams(("parallel", "arbitrary")),
        name="sb_attention",
    )(qkv, qkv, qkv)


def _pick(n, candidates):
    for c in candidates:
        if n % c == 0:
            return c
    return n


def kernel(x, mix_norm, a_w_in, a_conv_w, a_a_log, a_dt_bias, a_lb_logits, a_hgrn_norm,
           a_gdn_norm, a_w_out, c_w_qkv, c_w_o, mlp_norm, mlp_w1, mlp_w2, final_norm):
    bsz, t, d = x.shape
    assert bsz == 1
    depth = mix_norm.shape[0]
    nh = N_REC_HEADS
    kw = nh * HEAD_W
    xs = x.reshape(t, d)
    tm = _pick(t, (1024, 512, 256, 128, 64))
    tb_hgrn = _pick(t, (1024, 512, 256, 128, 64))
    tb_gdn = _pick(t, (256, 128, 64))
    sb_heads = d // HEAD_W
    sb_sub = _pick(t // SB_BLK, (8, 4, 2, 1))

    lb_all = jnp.cumsum(jax.nn.softmax(a_lb_logits.astype(F32), axis=0), axis=0)
    for layer in range(depth):
        j = layer // 2
        if layer % 2 == 0:
            w_in = a_w_in[j]
            n_gate = 2 * nh
            w_in_b = w_in.astype(BF16)
            w_main = jnp.concatenate([w_in_b[:, :7 * kw], w_in_b[:, 7 * kw + n_gate:]], axis=1)
            w_gate = jnp.pad(w_in_b[:, 7 * kw:7 * kw + n_gate], ((0, 0), (0, HEAD_W - n_gate)))
            proj, gates = norm_matmul(xs, mix_norm[layer], w_main, jnp.ones((8 * kw,), F32), F32,
                                      tm, _pick(8 * kw, (1024,)), w_extra=w_gate)
            gab_t = gates[:, :n_gate].T
            o_a = hgrn_heads(proj, lb_all[j], a_hgrn_norm[j], tb_hgrn)
            o_b = gdn_heads(proj, gab_t, a_conv_w[j], a_a_log[j], a_dt_bias[j], a_gdn_norm[j], tb_gdn)
            w_out = a_w_out[j].astype(BF16)
            xs = proj_residual([o_a, o_b], [w_out[:kw], w_out[kw:]], xs, tm, _pick(d, (1024,)))
        else:
            scale = jnp.concatenate([jnp.full((d,), HEAD_W ** -0.5, F32), jnp.ones((2 * d,), F32)])
            qkv, = norm_matmul(xs, mix_norm[layer], c_w_qkv[j].astype(BF16), scale, BF16,
                               tm, _pick(3 * d, (1024,)))
            o_c = sb_attention(qkv, sb_heads, sb_sub)
            xs = proj_residual([o_c], [c_w_o[j].astype(BF16)], xs, tm, _pick(d, (1024,)))
        last = layer == depth - 1
        xs = mlp_residual(xs, mlp_norm[layer], mlp_w1[layer].astype(BF16), mlp_w2[layer].astype(BF16),
                          final_norm, last, tm, _pick(mlp_w1.shape[2], (512,)))
    return xs.reshape(bsz, t, d)
```

```python
import functools

import jax
import jax.numpy as jnp
from jax import lax
from jax.experimental import pallas as pl
from jax.experimental.pallas import tpu as pltpu

F32 = jnp.float32
BF16 = jnp.bfloat16
EPS = 1e-6
HEAD_W = 128
CHUNK = 64
SUB = 16
CONV_W = 4
SB_BLK = 128
SB_SWEEP = 2
N_REC_HEADS = 8
WY_GROUP = 8
VMEM_LIMIT = 48 * 1024 * 1024
HI = lax.Precision.HIGHEST
EXP_ZERO_BELOW = -104.0


def _cparams(sem):
    return pltpu.CompilerParams(dimension_semantics=sem, vmem_limit_bytes=VMEM_LIMIT)


def _dot(a, b, precision=None):
    return jnp.dot(a, b, preferred_element_type=F32, precision=precision)


def _dot_nt(a, b, precision=None):
    return lax.dot_general(a, b, (((1,), (1,)), ((), ())),
                           preferred_element_type=F32, precision=precision)


def _rms(x, gain):
    return x * lax.rsqrt(jnp.mean(x * x, axis=-1, keepdims=True) + EPS) * gain


def _sigmoid(x):
    return 0.5 * jnp.tanh(0.5 * x) + 0.5


def _silu(x):
    return x * _sigmoid(x)


def _softplus(x):
    return jnp.maximum(x, 0.0) + jnp.log(1.0 + jnp.exp(-jnp.abs(x)))


def _norm_matmul_kernel(*refs, has_extra):
    if has_extra:
        x_ref, g_ref, w_ref, cs_ref, we_ref, o_ref, oe_ref, h_ref = refs
    else:
        x_ref, g_ref, w_ref, cs_ref, o_ref, h_ref = refs

    @pl.when(pl.program_id(1) == 0)
    def _():
        h_ref[...] = _rms(x_ref[...], g_ref[...]).astype(BF16)
        if has_extra:
            oe_ref[...] = _dot(h_ref[...], we_ref[...])

    acc = _dot(h_ref[...], w_ref[...]) * cs_ref[...]
    for c in range(o_ref.shape[0]):
        o_ref[c] = acc[:, c * HEAD_W:(c + 1) * HEAD_W].astype(o_ref.dtype)


def norm_matmul(x, gain, w, col_scale, out_dtype, tm, tn, w_extra=None):
    m, k = x.shape
    n = w.shape[1]
    has_extra = w_extra is not None
    in_specs = [
        pl.BlockSpec((tm, k), lambda i, j: (i, 0)),
        pl.BlockSpec((1, k), lambda i, j: (0, 0)),
        pl.BlockSpec((k, tn), lambda i, j: (0, j)),
        pl.BlockSpec((1, tn), lambda i, j: (0, j)),
    ]
    out_specs = [pl.BlockSpec((tn // HEAD_W, tm, HEAD_W), lambda i, j: (j, i, 0))]
    out_shape = [jax.ShapeDtypeStruct((n // HEAD_W, m, HEAD_W), out_dtype)]
    args = [x, gain.reshape(1, k), w, col_scale.reshape(1, n)]
    if has_extra:
        in_specs.append(pl.BlockSpec((k, HEAD_W), lambda i, j: (0, 0)))
        out_specs.append(pl.BlockSpec((tm, HEAD_W), lambda i, j: (i, 0)))
        out_shape.append(jax.ShapeDtypeStruct((m, HEAD_W), F32))
        args.append(w_extra)
    return pl.pallas_call(
        functools.partial(_norm_matmul_kernel, has_extra=has_extra),
        grid=(m // tm, n // tn),
        in_specs=in_specs,
        out_specs=out_specs,
        out_shape=out_shape,
        scratch_shapes=[pltpu.VMEM((tm, k), BF16)],
        compiler_params=_cparams(("parallel", "arbitrary")),
        name="norm_matmul",
    )(*args)


def _proj_residual_kernel(*refs, n_lhs):
    lhs = refs[:n_lhs]
    ws = refs[n_lhs:2 * n_lhs]
    x_ref, o_ref = refs[2 * n_lhs], refs[2 * n_lhs + 1]
    acc = x_ref[...]
    for a_ref, w_ref in zip(lhs, ws):
        acc = acc + _dot(a_ref[...], w_ref[...])
    o_ref[...] = acc


def proj_residual(lhs_list, w_list, x, tm, tn):
    m, n = x.shape
    n_lhs = len(lhs_list)
    in_specs = [pl.BlockSpec((tm, a.shape[1]), lambda i, j: (i, 0)) for a in lhs_list]
    in_specs += [pl.BlockSpec((w.shape[0], tn), lambda i, j: (0, j)) for w in w_list]
    in_specs += [pl.BlockSpec((tm, tn), lambda i, j: (i, j))]
    return pl.pallas_call(
        functools.partial(_proj_residual_kernel, n_lhs=n_lhs),
        grid=(m // tm, n // tn),
        in_specs=in_specs,
        out_specs=pl.BlockSpec((tm, tn), lambda i, j: (i, j)),
        out_shape=jax.ShapeDtypeStruct((m, n), F32),
        compiler_params=_cparams(("parallel", "arbitrary")),
        name="proj_residual",
    )(*lhs_list, *w_list, x)


def _mlp_kernel(x_ref, g_ref, w1_ref, w2_ref, fg_ref, o_ref, h_ref, *, final_norm):
    j = pl.program_id(1)

    @pl.when(j == 0)
    def _():
        x = x_ref[...]
        h_ref[...] = _rms(x, g_ref[...]).astype(BF16)
        o_ref[...] = x

    a = _dot(h_ref[...], w1_ref[...])
    a = jnp.square(jnp.maximum(a, 0.0)).astype(BF16)
    o_ref[...] += _dot(a, w2_ref[...])

    if final_norm:
        @pl.when(j == pl.num_programs(1) - 1)
        def _():
            o_ref[...] = _rms(o_ref[...], fg_ref[...])


def mlp_residual(x, gain, w1, w2, final_gain, final_norm, tm, tf):
    m, d = x.shape
    ff = w1.shape[1]
    return pl.pallas_call(
        functools.partial(_mlp_kernel, final_norm=final_norm),
        grid=(m // tm, ff // tf),
        in_specs=[
            pl.BlockSpec((tm, d), lambda i, j: (i, 0)),
            pl.BlockSpec((1, d), lambda i, j: (0, 0)),
            pl.BlockSpec((d, tf), lambda i, j: (0, j)),
            pl.BlockSpec((tf, d), lambda i, j: (j, 0)),
            pl.BlockSpec((1, d), lambda i, j: (0, 0)),
        ],
        out_specs=pl.BlockSpec((tm, d), lambda i, j: (i, 0)),
        out_shape=jax.ShapeDtypeStruct((m, d), F32),
        scratch_shapes=[pltpu.VMEM((tm, d), BF16)],
        compiler_params=_cparams(("parallel", "arbitrary")),
        name="mlp_residual",
    )(x, gain.reshape(1, d), w1, w2, final_gain.reshape(1, d))


def _hgrn_kernel(hq_ref, hf_ref, hi_ref, hg_ref, lb_ref, nw_ref, o_ref, st_ref, *, n_chunks):
    @pl.when(pl.program_id(1) == 0)
    def _():
        st_ref[...] = jnp.zeros_like(st_ref)

    chunks = range(n_chunks)
    n_sub = CHUNK // SUB
    lb = lb_ref[...]
    row = lax.broadcasted_iota(jnp.int32, (CHUNK, CHUNK), 0)
    col = lax.broadcasted_iota(jnp.int32, (CHUNK, CHUNK), 1)
    tril = (col <= row).astype(F32)

    hf = hf_ref[...]
    log_f = jnp.log(lb + (1.0 - lb) * _sigmoid(hf))
    k_all = (1.0 - lb) * _sigmoid(-hf)
    q_all = _silu(hq_ref[...])
    vb_all = hi_ref[...].astype(BF16)

    rows = [slice(c * CHUNK, (c + 1) * CHUNK) for c in chunks]
    q = [q_all[r] for r in rows]
    k = [k_all[r] for r in rows]
    vb = [vb_all[r] for r in rows]
    b = [_dot(tril, log_f[r], precision=HI) for r in rows]
    o_intra = [[] for _ in chunks]
    for i in range(n_sub):
        lo, hi = i * SUB, (i + 1) * SUB
        srow = lax.broadcasted_iota(jnp.int32, (SUB, hi), 0)
        scol = lax.broadcasted_iota(jnp.int32, (SUB, hi), 1)
        causal = scol <= srow + lo
        b_ref = [b[c][lo:lo + 1, :] for c in chunks]
        qs = [(q[c][lo:hi] * jnp.exp(b[c][lo:hi] - b_ref[c])).astype(BF16) for c in chunks]
        ks = [(k[c][:hi] * jnp.exp(b_ref[c] - b[c][:hi])).astype(BF16) for c in chunks]
        att = [jnp.where(causal, _dot_nt(qs[c], ks[c]), 0.0).astype(BF16) for c in chunks]
        for c in chunks:
            o_intra[c].append(_dot(att[c], vb[c][:hi]))
    b_last = [b[c][CHUNK - 1:CHUNK, :] for c in chunks]
    qd = [(q[c] * jnp.exp(b[c])).astype(BF16) for c in chunks]
    kd = [(k[c] * jnp.exp(b_last[c] - b[c])).astype(BF16) for c in chunks]
    p = [_dot(vb[c].T, kd[c]) for c in chunks]
    d = [jnp.exp(b_last[c]) for c in chunks]
    gate = nw_ref[...] * _silu(hg_ref[...])

    st = st_ref[...]
    for c in chunks:
        o = jnp.concatenate(o_intra[c], axis=0) + _dot_nt(qd[c], st.astype(BF16))
        st = st * d[c] + p[c]
        y = o * lax.rsqrt(jnp.mean(o * o, axis=-1, keepdims=True) + EPS)
        o_ref[rows[c], :] = (y * gate[rows[c]]).astype(o_ref.dtype)
    st_ref[...] = st


def hgrn_heads(proj, lb, norm_w, tb):
    t = proj.shape[1]
    nh = N_REC_HEADS

    def col(block0):
        return pl.BlockSpec((None, tb, HEAD_W), lambda h, i: (block0 + h, i, 0))

    return pl.pallas_call(
        functools.partial(_hgrn_kernel, n_chunks=tb // CHUNK),
        grid=(nh, t // tb),
        in_specs=[col(0), col(nh), col(2 * nh), col(3 * nh),
                  pl.BlockSpec((None, 1, HEAD_W), lambda h, i: (h, 0, 0)),
                  pl.BlockSpec((1, HEAD_W), lambda h, i: (0, 0))],
        out_specs=pl.BlockSpec((tb, HEAD_W), lambda h, i: (i, h)),
        out_shape=jax.ShapeDtypeStruct((t, nh * HEAD_W), BF16),
        scratch_shapes=[pltpu.VMEM((HEAD_W, HEAD_W), F32)],
        compiler_params=_cparams(("parallel", "arbitrary")),
        name="hgrn_heads",
    )(proj, proj, proj, proj, lb.reshape(nh, 1, HEAD_W), norm_w.reshape(1, HEAD_W))


def _gdn_kernel(alog_ref, dtb_ref, gq_ref, gk_ref, gv_ref, gg_ref, gab_ref, cw_ref, nw_ref, o_ref,
                s_ref, tail_ref, q_s, k_s, v_s, g_s, beta_s, u_s, wq_s, kdt_s, qk_s, gl_s, *, n_chunks):
    tb = n_chunks * CHUNK
    nh = N_REC_HEADS

    @pl.when(pl.program_id(0) == 0)
    def _():
        s_ref[...] = jnp.zeros_like(s_ref)
        tail_ref[...] = jnp.zeros_like(tail_ref)

    def conv_silu(u_ref, slot, h):
        cols = slice(h * HEAD_W, (h + 1) * HEAD_W)
        u = u_ref[h]
        ext = jnp.concatenate([tail_ref[slot, :, cols], u], axis=0)
        w = cw_ref[:, slot * nh * HEAD_W + h * HEAD_W:slot * nh * HEAD_W + (h + 1) * HEAD_W]
        acc = u * w[CONV_W - 1:CONV_W, :]
        for j in range(CONV_W - 1):
            acc = acc + pltpu.roll(ext, CONV_W - 1 - j, axis=0)[8:, :] * w[j:j + 1, :]
        tail_ref[slot, :, cols] = u[tb - 8:, :]
        return _silu(acc)

    for h in range(nh):
        qh = conv_silu(gq_ref, 0, h)
        kh = conv_silu(gk_ref, 1, h)
        q_s[h] = qh * lax.rsqrt(jnp.sum(qh * qh, axis=-1, keepdims=True) + EPS) * (HEAD_W ** -0.5)
        k_s[h] = kh * lax.rsqrt(jnp.sum(kh * kh, axis=-1, keepdims=True) + EPS)
        v_s[h] = conv_silu(gv_ref, 2, h)
    gab = gab_ref[...].T
    g = -jnp.exp(alog_ref[...]) * _softplus(gab[:nh, :] + dtb_ref[...])
    brow = lax.broadcasted_iota(jnp.int32, (tb, tb), 0)
    bcol = lax.broadcasted_iota(jnp.int32, (tb, tb), 1)
    same_chunk_upto = jnp.logical_and(brow <= bcol, brow // CHUNK == bcol // CHUNK).astype(F32)
    g_s[...] = _dot(g, same_chunk_upto, precision=HI)
    beta_s[...] = _sigmoid(gab[nh:2 * nh, :])

    row = lax.broadcasted_iota(jnp.int32, (CHUNK, CHUNK), 0)
    col = lax.broadcasted_iota(jnp.int32, (CHUNK, CHUNK), 1)
    tril = col <= row
    strict = col < row
    eye = col == row
    eye_f = eye.astype(F32)

    def wy_heads(hg, carry):
        items = [(hg * WY_GROUP + dh, c) for dh in range(WY_GROUP) for c in range(n_chunks)]
        n = range(len(items))
        rows = [slice(c * CHUNK, (c + 1) * CHUNK) for _, c in items]
        q = [q_s[h, rows[x], :] for x, (h, _) in enumerate(items)]
        k = [k_s[h, rows[x], :] for x, (h, _) in enumerate(items)]
        gc_all = [g_s[pl.ds(hg * WY_GROUP + dh, 1), :] for dh in range(WY_GROUP)]
        beta_all = [beta_s[pl.ds(hg * WY_GROUP + dh, 1), :] for dh in range(WY_GROUP)]
        gc_row = [jnp.broadcast_to(gc_all[x // n_chunks][:, rows[x]], (CHUNK, CHUNK)) for x in n]
        beta_b = [jnp.broadcast_to(beta_all[x // n_chunks][:, rows[x]], (CHUNK, CHUNK)) for x in n]
        gc_col = [jnp.sum(jnp.where(eye, gc_row[x], 0.0), axis=-1, keepdims=True) for x in n]
        beta_col = [jnp.sum(jnp.where(eye, beta_b[x], 0.0), axis=-1, keepdims=True) for x in n]
        decay = [jnp.where(tril, jnp.exp(jnp.minimum(gc_col[x] - gc_row[x], 0.0)), 0.0) for x in n]
        kb = [k[x] * beta_col[x] for x in n]
        prod = [_dot_nt(jnp.concatenate([kb[x], q[x]], axis=0).astype(BF16), k[x].astype(BF16))
                for x in n]
        n1 = [jnp.where(strict, -prod[x][:CHUNK] * decay[x], 0.0) for x in n]
        inv = [eye_f + n1[x] for x in n]
        npow = [n1[x].astype(BF16) for x in n]
        for _ in range(5):
            npow = [_dot(npow[x], npow[x]).astype(BF16) for x in n]
            inv = [inv[x] + _dot(inv[x].astype(BF16), npow[x]) for x in n]
        e_gc = [jnp.exp(gc_col[x]) for x in n]
        rhs = [jnp.concatenate([v_s[h, rows[x], :] * beta_col[x], kb[x] * e_gc[x]], axis=1)
               for x, (h, _) in enumerate(items)]
        sol = [_dot(inv[x].astype(BF16), rhs[x].astype(BF16)) for x in n]
        for x, (h, c) in enumerate(items):
            g_last = gc_col[x][CHUNK - 1:CHUNK, :]
            u_s[h, rows[x], :] = sol[x][:, :HEAD_W]
            wq_s[h, c, :CHUNK, :] = sol[x][:, HEAD_W:].astype(BF16)
            wq_s[h, c, CHUNK:, :] = (q[x] * e_gc[x]).astype(BF16)
            kdt_s[h, c] = (k[x] * jnp.exp(g_last - gc_col[x])).T.astype(BF16)
            qk_s[h, c] = (prod[x][CHUNK:] * decay[x]).astype(BF16)
            gl_s[h, c] = jnp.broadcast_to(jnp.exp(g_last), (8, HEAD_W))
        return carry

    lax.fori_loop(0, nh // WY_GROUP, wy_heads, 0)

    heads = range(nh)
    s = [s_ref[h] for h in heads]
    for c in range(n_chunks):
        lo, hi = c * CHUNK, (c + 1) * CHUNK
        ws = [_dot(wq_s[h, c], s[h].astype(BF16)) for h in heads]
        v_new = [(u_s[h, lo:hi, :] - ws[h][:CHUNK]).astype(BF16) for h in heads]
        o = [ws[h][CHUNK:] + _dot(qk_s[h, c], v_new[h]) for h in heads]
        s = [s[h] * gl_s[h, c][:1, :] + _dot(kdt_s[h, c], v_new[h]) for h in heads]
        for h in heads:
            y = o[h] * lax.rsqrt(jnp.mean(o[h] * o[h], axis=-1, keepdims=True) + EPS)
            o_ref[lo:hi, h * HEAD_W:(h + 1) * HEAD_W] = (
                y * nw_ref[...] * _silu(gg_ref[h, lo:hi, :])).astype(o_ref.dtype)
    for h in heads:
        s_ref[h] = s[h]


def gdn_heads(proj, gates, conv_w, a_log, dt_bias, norm_w, tb):
    t = proj.shape[1]
    nh = N_REC_HEADS
    gw = nh * HEAD_W
    n_chunks = tb // CHUNK

    def group(block):
        return pl.BlockSpec((nh, tb, HEAD_W), lambda i: (block, i, 0))

    def whole(a):
        return pl.BlockSpec(a.shape, lambda i: (0,) * a.ndim)

    a_log = a_log.reshape(nh, 1)
    dt_bias = dt_bias.reshape(nh, 1)
    norm_w = norm_w.reshape(1, HEAD_W)
    return pl.pallas_call(
        functools.partial(_gdn_kernel, n_chunks=n_chunks),
        grid=(t // tb,),
        in_specs=[whole(a_log), whole(dt_bias), group(4), group(5), group(6), group(7),
                  pl.BlockSpec((tb, HEAD_W), lambda i: (i, 0)), whole(conv_w), whole(norm_w)],
        out_specs=pl.BlockSpec((tb, gw), lambda i: (i, 0)),
        out_shape=jax.ShapeDtypeStruct((t, gw), BF16),
        scratch_shapes=[pltpu.VMEM((nh, HEAD_W, HEAD_W), F32),
                        pltpu.VMEM((3, 8, gw), F32),
                        pltpu.VMEM((nh, tb, HEAD_W), F32),
                        pltpu.VMEM((nh, tb, HEAD_W), F32),
                        pltpu.VMEM((nh, tb, HEAD_W), F32),
                        pltpu.VMEM((nh, tb), F32),
                        pltpu.VMEM((nh, tb), F32),
                        pltpu.VMEM((nh, tb, HEAD_W), F32),
                        pltpu.VMEM((nh, n_chunks, 2 * CHUNK, HEAD_W), BF16),
                        pltpu.VMEM((nh, n_chunks, HEAD_W, CHUNK), BF16),
                        pltpu.VMEM((nh, n_chunks, CHUNK, CHUNK), BF16),
                        pltpu.VMEM((nh, n_chunks, 8, HEAD_W), F32)],
        compiler_params=_cparams(("arbitrary",)),
        name="gdn_heads",
    )(a_log, dt_bias, proj, proj, proj, proj, gates, conv_w, norm_w)


def _sb_kernel(q_ref, k_ref, v_ref, o_ref, acc_ref, c_ref, *, n_sub):
    blk = SB_BLK
    qb0 = pl.program_id(1) * n_sub
    row = lax.broadcasted_iota(jnp.int32, (2 * blk, 2 * blk), 0)
    col = lax.broadcasted_iota(jnp.int32, (2 * blk, 2 * blk), 1)
    later2 = (row > col).astype(BF16)
    later = later2[:blk, :blk]
    causal = (col < row)[:blk, :blk]

    def tiles(subs, s, masked, watch=None, width=1):
        n = range(len(subs))
        kw = width * blk
        start = [pl.multiple_of((qb0 + i - s - (width - 1)) * blk, blk) for i in subs]
        z = [_dot_nt(q_ref[i * blk:(i + 1) * blk, :], k_ref[pl.ds(start[x], kw), :])
             for x, i in enumerate(subs)]
        sp = [_softplus(z[x]) for x in n]
        log_1m = [jnp.where(causal, -sp[x], 0.0) if masked else -sp[x] for x in n]
        after = [_dot(log_1m[x].astype(BF16), later if width == 1 else later2)
                 for x in n]
        c = [c_ref[i] for i in subs]
        a = [jnp.exp((z[x] - sp[x]) + after[x] + c[x]) for x in n]
        if masked:
            a = [jnp.where(causal, a[x], 0.0) for x in n]
        c_new = [c[x] + after[x][:, :1] + log_1m[x][:, :1] for x in n]
        for x, i in enumerate(subs):
            acc_ref[i] += _dot(a[x].astype(BF16), v_ref[pl.ds(start[x], kw), :])
            c_ref[i] = c_new[x]
        watched = [c_new[x] for x, i in enumerate(subs) if watch is None or i in watch]
        if not watched:
            return jnp.float32(-jnp.inf)
        return jnp.max(functools.reduce(jnp.maximum, watched))

    all_subs = list(range(n_sub))
    acc_ref[...] = jnp.zeros_like(acc_ref)
    c_ref[...] = jnp.zeros_like(c_ref)
    c_max = tiles(all_subs, 0, True)

    @pl.when(qb0 > 0)
    def _():
        s_end, c_left = lax.while_loop(
            lambda carry: jnp.logical_and(carry[0] + (SB_SWEEP - 1) <= qb0, carry[1] > EXP_ZERO_BELOW),
            lambda carry: (carry[0] + SB_SWEEP, tiles(all_subs, carry[0], False, width=SB_SWEEP)),
            (jnp.int32(1), c_max))

        @pl.when(c_left > EXP_ZERO_BELOW)
        def _():
            for i in range(1, n_sub):
                lax.while_loop(
                    lambda carry, i=i: jnp.logical_and(carry[0] <= qb0 + i, carry[1] > EXP_ZERO_BELOW),
                    lambda carry, i=i: (carry[0] + 1, tiles([i], carry[0], False)),
                    (s_end, jnp.max(c_ref[i])))

    def first_block_sweep(s, c_left):
        if s < n_sub:
            @pl.when(c_left > EXP_ZERO_BELOW)
            def _():
                subs = list(range(s, n_sub))
                first_block_sweep(s + 1, tiles(subs, s, False, watch=subs[1:]))

    @pl.when(qb0 == 0)
    def _():
        first_block_sweep(1, c_max)

    for i in range(n_sub):
        o_ref[i * blk:(i + 1) * blk, :] = acc_ref[i].astype(o_ref.dtype)


def sb_attention(qkv, n_heads, n_sub):
    t = qkv.shape[1]
    qb = n_sub * SB_BLK
    return pl.pallas_call(
        functools.partial(_sb_kernel, n_sub=n_sub),
        grid=(n_heads, t // qb),
        in_specs=[pl.BlockSpec((None, qb, HEAD_W), lambda h, i: (h, i, 0)),
                  pl.BlockSpec((None, t, HEAD_W), lambda h, i: (n_heads + h, 0, 0)),
                  pl.BlockSpec((None, t, HEAD_W), lambda h, i: (2 * n_heads + h, 0, 0))],
        out_specs=pl.BlockSpec((qb, HEAD_W), lambda h, i: (i, h)),
        out_shape=jax.ShapeDtypeStruct((t, n_heads * HEAD_W), BF16),
        scratch_shapes=[pltpu.VMEM((n_sub, SB_BLK, HEAD_W), F32),
                        pltpu.VMEM((n_sub, SB_BLK, 1), F32)],
        compiler_params=_cparams(("parallel", "arbitrary")),
        name="sb_attention",
    )(qkv, qkv, qkv)


def _pick(n, candidates):
    for c in candidates:
        if n % c == 0:
            return c
    return n


def kernel(x, mix_norm, a_w_in, a_conv_w, a_a_log, a_dt_bias, a_lb_logits, a_hgrn_norm,
           a_gdn_norm, a_w_out, c_w_qkv, c_w_o, mlp_norm, mlp_w1, mlp_w2, final_norm):
    bsz, t, d = x.shape
    assert bsz == 1
    depth = mix_norm.shape[0]
    nh = N_REC_HEADS
    kw = nh * HEAD_W
    xs = x.reshape(t, d)
    tm = _pick(t, (1024, 512, 256, 128, 64))
    tm_proj = _pick(t, (512, 256, 128, 64))
    tb_hgrn = _pick(t, (1024, 512, 256, 128, 64))
    tb_gdn = _pick(t, (256, 128, 64))
    sb_heads = d // HEAD_W
    sb_sub = _pick(t // SB_BLK, (8, 4, 2, 1))

    lb_all = jnp.cumsum(jax.nn.softmax(a_lb_logits.astype(F32), axis=0), axis=0)
    for layer in range(depth):
        j = layer // 2
        if layer % 2 == 0:
            w_in = a_w_in[j]
            n_gate = 2 * nh
            w_in_b = w_in.astype(BF16)
            w_main = jnp.concatenate([w_in_b[:, :7 * kw], w_in_b[:, 7 * kw + n_gate:]], axis=1)
            w_gate = jnp.pad(w_in_b[:, 7 * kw:7 * kw + n_gate], ((0, 0), (0, HEAD_W - n_gate)))
            proj, gates = norm_matmul(xs, mix_norm[layer], w_main, jnp.ones((8 * kw,), F32), F32,
                                      tm, _pick(8 * kw, (1024,)), w_extra=w_gate)
            o_a = hgrn_heads(proj, lb_all[j], a_hgrn_norm[j], tb_hgrn)
            o_b = gdn_heads(proj, gates, a_conv_w[j], a_a_log[j], a_dt_bias[j], a_gdn_norm[j], tb_gdn)
            w_out = a_w_out[j].astype(BF16)
            xs = proj_residual([o_a, o_b], [w_out[:kw], w_out[kw:]], xs, tm_proj, d)
        else:
            scale = jnp.concatenate([jnp.full((d,), HEAD_W ** -0.5, F32), jnp.ones((2 * d,), F32)])
            qkv, = norm_matmul(xs, mix_norm[layer], c_w_qkv[j].astype(BF16), scale, BF16,
                               tm, _pick(3 * d, (1024,)))
            o_c = sb_attention(qkv, sb_heads, sb_sub)
            xs = proj_residual([o_c], [c_w_o[j].astype(BF16)], xs, tm_proj, d)
        last = layer == depth - 1
        xs = mlp_residual(xs, mlp_norm[layer], mlp_w1[layer].astype(BF16), mlp_w2[layer].astype(BF16),
                          final_norm, last, tm, _pick(mlp_w1.shape[2], (512,)))
    return xs.reshape(bsz, t, d)
```

```python
import functools

import jax
import jax.numpy as jnp
from jax import lax
from jax.experimental import pallas as pl
from jax.experimental.pallas import tpu as pltpu

F32 = jnp.float32
BF16 = jnp.bfloat16
EPS = 1e-6
HEAD_W = 128
CHUNK = 64
SUB = 16
CONV_W = 4
SB_BLK = 128
SB_SWEEP = 2
N_REC_HEADS = 8
WY_GROUP = 8
VMEM_LIMIT = 56 * 1024 * 1024
HI = lax.Precision.HIGHEST
EXP_ZERO_BELOW = -104.0


def _cparams(sem):
    return pltpu.CompilerParams(dimension_semantics=sem, vmem_limit_bytes=VMEM_LIMIT)


def _dot(a, b, precision=None):
    return jnp.dot(a, b, preferred_element_type=F32, precision=precision)


def _dot_nt(a, b, precision=None):
    return lax.dot_general(a, b, (((1,), (1,)), ((), ())),
                           preferred_element_type=F32, precision=precision)


def _rms(x, gain):
    return x * lax.rsqrt(jnp.mean(x * x, axis=-1, keepdims=True) + EPS) * gain


def _sigmoid(x):
    return 0.5 * jnp.tanh(0.5 * x) + 0.5


def _silu(x):
    return x * _sigmoid(x)


def _softplus(x):
    return jnp.maximum(x, 0.0) + jnp.log(1.0 + jnp.exp(-jnp.abs(x)))


def _norm_matmul_kernel(*refs, parts, has_extra):
    n_parts = len(parts)
    x_ref, g_ref = refs[:2]
    w_refs = refs[2:2 + n_parts]
    rest = refs[2 + n_parts:]
    if has_extra:
        cs_ref, we_ref, o_ref, oe_ref, h_ref = rest
    else:
        cs_ref, o_ref, h_ref = rest
    j = pl.program_id(1)

    @pl.when(j == 0)
    def _():
        h_ref[...] = _rms(x_ref[...], g_ref[...]).astype(BF16)
        if has_extra:
            oe_ref[...] = _dot(h_ref[...], we_ref[...])

    def emit(w_ref):
        acc = _dot(h_ref[...], w_ref[...]) * cs_ref[...]
        for c in range(o_ref.shape[0]):
            o_ref[c] = acc[:, c * HEAD_W:(c + 1) * HEAD_W].astype(o_ref.dtype)

    if n_parts == 1:
        emit(w_refs[0])
    else:
        for w_ref, (lo, hi) in zip(w_refs, parts):
            pl.when(jnp.logical_and(j >= lo, j < hi))(functools.partial(emit, w_ref))


def norm_matmul(x, gain, w_parts, col_scale, out_dtype, tm, tn, w_extra=None):
    m, k = x.shape
    has_extra = w_extra is not None
    parts, lo = [], 0
    for _, n_tiles in w_parts:
        parts.append((lo, lo + n_tiles))
        lo += n_tiles
    n = lo * tn
    in_specs = [pl.BlockSpec((tm, k), lambda i, j: (i, 0)),
                pl.BlockSpec((1, k), lambda i, j: (0, 0))]
    for lo, hi in parts:
        in_specs.append(pl.BlockSpec((k, tn), lambda i, j, lo=lo, hi=hi: (0, jnp.clip(j - lo, 0, hi - lo - 1))))
    in_specs.append(pl.BlockSpec((1, tn), lambda i, j: (0, j)))
    out_specs = [pl.BlockSpec((tn // HEAD_W, tm, HEAD_W), lambda i, j: (j, i, 0))]
    out_shape = [jax.ShapeDtypeStruct((n // HEAD_W, m, HEAD_W), out_dtype)]
    args = [x, gain.reshape(1, k)] + [w for w, _ in w_parts] + [col_scale.reshape(1, n)]
    if has_extra:
        in_specs.append(pl.BlockSpec((k, HEAD_W), lambda i, j: (0, 0)))
        out_specs.append(pl.BlockSpec((tm, HEAD_W), lambda i, j: (i, 0)))
        out_shape.append(jax.ShapeDtypeStruct((m, HEAD_W), F32))
        args.append(w_extra)
    return pl.pallas_call(
        functools.partial(_norm_matmul_kernel, parts=tuple(parts), has_extra=has_extra),
        grid=(m // tm, n // tn),
        in_specs=in_specs,
        out_specs=out_specs,
        out_shape=out_shape,
        scratch_shapes=[pltpu.VMEM((tm, k), BF16)],
        compiler_params=_cparams(("parallel", "arbitrary")),
        name="norm_matmul",
    )(*args)


def _proj_residual_kernel(*refs, n_lhs):
    lhs = refs[:n_lhs]
    ws = refs[n_lhs:2 * n_lhs]
    x_ref, o_ref = refs[2 * n_lhs], refs[2 * n_lhs + 1]
    acc = x_ref[...]
    for a_ref, w_ref in zip(lhs, ws):
        acc = acc + _dot(a_ref[...], w_ref[...])
    o_ref[...] = acc


def proj_residual(lhs_list, w_list, x, tm, tn):
    m, n = x.shape
    n_lhs = len(lhs_list)
    in_specs = [pl.BlockSpec((tm, a.shape[1]), lambda i, j: (i, 0)) for a in lhs_list]
    in_specs += [pl.BlockSpec((w.shape[0], tn), lambda i, j: (0, j)) for w in w_list]
    in_specs += [pl.BlockSpec((tm, tn), lambda i, j: (i, j))]
    return pl.pallas_call(
        functools.partial(_proj_residual_kernel, n_lhs=n_lhs),
        grid=(m // tm, n // tn),
        in_specs=in_specs,
        out_specs=pl.BlockSpec((tm, tn), lambda i, j: (i, j)),
        out_shape=jax.ShapeDtypeStruct((m, n), F32),
        compiler_params=_cparams(("parallel", "arbitrary")),
        name="proj_residual",
    )(*lhs_list, *w_list, x)


def _mlp_kernel(x_ref, g_ref, w1_ref, w2_ref, fg_ref, o_ref, h_ref, *, final_norm):
    j = pl.program_id(1)

    @pl.when(j == 0)
    def _():
        x = x_ref[...]
        h_ref[...] = _rms(x, g_ref[...]).astype(BF16)
        o_ref[...] = x

    a = _dot(h_ref[...], w1_ref[...])
    a = jnp.square(jnp.maximum(a, 0.0)).astype(BF16)
    o_ref[...] += _dot(a, w2_ref[...])

    if final_norm:
        @pl.when(j == pl.num_programs(1) - 1)
        def _():
            o_ref[...] = _rms(o_ref[...], fg_ref[...])


def mlp_residual(x, gain, w1, w2, final_gain, final_norm, tm, tf):
    m, d = x.shape
    ff = w1.shape[1]
    return pl.pallas_call(
        functools.partial(_mlp_kernel, final_norm=final_norm),
        grid=(m // tm, ff // tf),
        in_specs=[
            pl.BlockSpec((tm, d), lambda i, j: (i, 0)),
            pl.BlockSpec((1, d), lambda i, j: (0, 0)),
            pl.BlockSpec((d, tf), lambda i, j: (0, j)),
            pl.BlockSpec((tf, d), lambda i, j: (j, 0)),
            pl.BlockSpec((1, d), lambda i, j: (0, 0)),
        ],
        out_specs=pl.BlockSpec((tm, d), lambda i, j: (i, 0)),
        out_shape=jax.ShapeDtypeStruct((m, d), F32),
        scratch_shapes=[pltpu.VMEM((tm, d), BF16)],
        compiler_params=_cparams(("parallel", "arbitrary")),
        name="mlp_residual",
    )(x, gain.reshape(1, d), w1, w2, final_gain.reshape(1, d))


def _hgrn_kernel(hq_ref, hf_ref, hi_ref, hg_ref, lb_ref, nw_ref, o_ref, st_ref, *, n_chunks):
    @pl.when(pl.program_id(1) == 0)
    def _():
        st_ref[...] = jnp.zeros_like(st_ref)

    chunks = range(n_chunks)
    n_sub = CHUNK // SUB
    lb = lb_ref[...]
    row = lax.broadcasted_iota(jnp.int32, (CHUNK, CHUNK), 0)
    col = lax.broadcasted_iota(jnp.int32, (CHUNK, CHUNK), 1)
    tril = (col <= row).astype(F32)

    hf = hf_ref[...]
    log_f = jnp.log(lb + (1.0 - lb) * _sigmoid(hf))
    k_all = (1.0 - lb) * _sigmoid(-hf)
    q_all = _silu(hq_ref[...])
    vb_all = hi_ref[...].astype(BF16)

    rows = [slice(c * CHUNK, (c + 1) * CHUNK) for c in chunks]
    q = [q_all[r] for r in rows]
    k = [k_all[r] for r in rows]
    vb = [vb_all[r] for r in rows]
    b = [_dot(tril, log_f[r], precision=HI) for r in rows]
    o_intra = [[] for _ in chunks]
    for i in range(n_sub):
        lo, hi = i * SUB, (i + 1) * SUB
        srow = lax.broadcasted_iota(jnp.int32, (SUB, hi), 0)
        scol = lax.broadcasted_iota(jnp.int32, (SUB, hi), 1)
        causal = scol <= srow + lo
        b_ref = [b[c][lo:lo + 1, :] for c in chunks]
        qs = [(q[c][lo:hi] * jnp.exp(b[c][lo:hi] - b_ref[c])).astype(BF16) for c in chunks]
        ks = [(k[c][:hi] * jnp.exp(b_ref[c] - b[c][:hi])).astype(BF16) for c in chunks]
        att = [jnp.where(causal, _dot_nt(qs[c], ks[c]), 0.0).astype(BF16) for c in chunks]
        for c in chunks:
            o_intra[c].append(_dot(att[c], vb[c][:hi]))
    b_last = [b[c][CHUNK - 1:CHUNK, :] for c in chunks]
    qd = [(q[c] * jnp.exp(b[c])).astype(BF16) for c in chunks]
    kd = [(k[c] * jnp.exp(b_last[c] - b[c])).astype(BF16) for c in chunks]
    p = [_dot(vb[c].T, kd[c]) for c in chunks]
    d = [jnp.exp(b_last[c]) for c in chunks]
    gate = nw_ref[...] * _silu(hg_ref[...])

    st = st_ref[...]
    for c in chunks:
        o = jnp.concatenate(o_intra[c], axis=0) + _dot_nt(qd[c], st.astype(BF16))
        st = st * d[c] + p[c]
        y = o * lax.rsqrt(jnp.mean(o * o, axis=-1, keepdims=True) + EPS)
        o_ref[rows[c], :] = (y * gate[rows[c]]).astype(o_ref.dtype)
    st_ref[...] = st


def hgrn_heads(proj, lb, norm_w, tb):
    t = proj.shape[1]
    nh = N_REC_HEADS

    def col(block0):
        return pl.BlockSpec((None, tb, HEAD_W), lambda h, i: (block0 + h, i, 0))

    return pl.pallas_call(
        functools.partial(_hgrn_kernel, n_chunks=tb // CHUNK),
        grid=(nh, t // tb),
        in_specs=[col(0), col(nh), col(2 * nh), col(3 * nh),
                  pl.BlockSpec((None, 1, HEAD_W), lambda h, i: (h, 0, 0)),
                  pl.BlockSpec((1, HEAD_W), lambda h, i: (0, 0))],
        out_specs=pl.BlockSpec((tb, HEAD_W), lambda h, i: (i, h)),
        out_shape=jax.ShapeDtypeStruct((t, nh * HEAD_W), BF16),
        scratch_shapes=[pltpu.VMEM((HEAD_W, HEAD_W), F32)],
        compiler_params=_cparams(("parallel", "arbitrary")),
        name="hgrn_heads",
    )(proj, proj, proj, proj, lb.reshape(nh, 1, HEAD_W), norm_w.reshape(1, HEAD_W))


def _gdn_kernel(alog_ref, dtb_ref, gq_ref, gk_ref, gv_ref, gg_ref, gab_ref, cw_ref, nw_ref, o_ref,
                s_ref, tail_ref, q_s, k_s, v_s, g_s, beta_s, u_s, wq_s, kdt_s, qk_s, gl_s, *, n_chunks):
    tb = n_chunks * CHUNK
    nh = N_REC_HEADS

    @pl.when(pl.program_id(0) == 0)
    def _():
        s_ref[...] = jnp.zeros_like(s_ref)
        tail_ref[...] = jnp.zeros_like(tail_ref)

    def conv_silu(u_ref, slot, h):
        cols = slice(h * HEAD_W, (h + 1) * HEAD_W)
        u = u_ref[h]
        ext = jnp.concatenate([tail_ref[slot, :, cols], u], axis=0)
        w = cw_ref[:, slot * nh * HEAD_W + h * HEAD_W:slot * nh * HEAD_W + (h + 1) * HEAD_W]
        acc = u * w[CONV_W - 1:CONV_W, :]
        for j in range(CONV_W - 1):
            acc = acc + pltpu.roll(ext, CONV_W - 1 - j, axis=0)[8:, :] * w[j:j + 1, :]
        tail_ref[slot, :, cols] = u[tb - 8:, :]
        return _silu(acc)

    for h in range(nh):
        qh = conv_silu(gq_ref, 0, h)
        kh = conv_silu(gk_ref, 1, h)
        q_s[h] = qh * lax.rsqrt(jnp.sum(qh * qh, axis=-1, keepdims=True) + EPS) * (HEAD_W ** -0.5)
        k_s[h] = kh * lax.rsqrt(jnp.sum(kh * kh, axis=-1, keepdims=True) + EPS)
        v_s[h] = conv_silu(gv_ref, 2, h)
    gab = gab_ref[...].T
    g = -jnp.exp(alog_ref[...]) * _softplus(gab[:nh, :] + dtb_ref[...])
    brow = lax.broadcasted_iota(jnp.int32, (tb, tb), 0)
    bcol = lax.broadcasted_iota(jnp.int32, (tb, tb), 1)
    same_chunk_upto = jnp.logical_and(brow <= bcol, brow // CHUNK == bcol // CHUNK).astype(F32)
    g_s[...] = _dot(g, same_chunk_upto, precision=HI)
    beta_s[...] = _sigmoid(gab[nh:2 * nh, :])

    row = lax.broadcasted_iota(jnp.int32, (CHUNK, CHUNK), 0)
    col = lax.broadcasted_iota(jnp.int32, (CHUNK, CHUNK), 1)
    tril = col <= row
    strict = col < row
    eye = col == row
    eye_f = eye.astype(F32)

    def wy_heads(hg, carry):
        items = [(hg * WY_GROUP + dh, c) for dh in range(WY_GROUP) for c in range(n_chunks)]
        n = range(len(items))
        rows = [slice(c * CHUNK, (c + 1) * CHUNK) for _, c in items]
        q = [q_s[h, rows[x], :] for x, (h, _) in enumerate(items)]
        k = [k_s[h, rows[x], :] for x, (h, _) in enumerate(items)]
        gc_all = [g_s[pl.ds(hg * WY_GROUP + dh, 1), :] for dh in range(WY_GROUP)]
        beta_all = [beta_s[pl.ds(hg * WY_GROUP + dh, 1), :] for dh in range(WY_GROUP)]
        gc_row = [jnp.broadcast_to(gc_all[x // n_chunks][:, rows[x]], (CHUNK, CHUNK)) for x in n]
        beta_b = [jnp.broadcast_to(beta_all[x // n_chunks][:, rows[x]], (CHUNK, CHUNK)) for x in n]
        gc_col = [jnp.sum(jnp.where(eye, gc_row[x], 0.0), axis=-1, keepdims=True) for x in n]
        beta_col = [jnp.sum(jnp.where(eye, beta_b[x], 0.0), axis=-1, keepdims=True) for x in n]
        decay = [jnp.where(tril, jnp.exp(jnp.minimum(gc_col[x] - gc_row[x], 0.0)), 0.0) for x in n]
        kb = [k[x] * beta_col[x] for x in n]
        prod = [_dot_nt(jnp.concatenate([kb[x], q[x]], axis=0).astype(BF16), k[x].astype(BF16))
                for x in n]
        n1 = [jnp.where(strict, -prod[x][:CHUNK] * decay[x], 0.0) for x in n]
        inv = [eye_f + n1[x] for x in n]
        npow = [n1[x].astype(BF16) for x in n]
        for _ in range(5):
            npow = [_dot(npow[x], npow[x]).astype(BF16) for x in n]
            inv = [inv[x] + _dot(inv[x].astype(BF16), npow[x]) for x in n]
        e_gc = [jnp.exp(gc_col[x]) for x in n]
        rhs = [jnp.concatenate([v_s[h, rows[x], :] * beta_col[x], kb[x] * e_gc[x]], axis=1)
               for x, (h, _) in enumerate(items)]
        sol = [_dot(inv[x].astype(BF16), rhs[x].astype(BF16)) for x in n]
        for x, (h, c) in enumerate(items):
            g_last = gc_col[x][CHUNK - 1:CHUNK, :]
            u_s[h, rows[x], :] = sol[x][:, :HEAD_W]
            wq_s[h, c, :CHUNK, :] = sol[x][:, HEAD_W:].astype(BF16)
            wq_s[h, c, CHUNK:, :] = (q[x] * e_gc[x]).astype(BF16)
            kdt_s[h, c] = (k[x] * jnp.exp(g_last - gc_col[x])).T.astype(BF16)
            qk_s[h, c] = (prod[x][CHUNK:] * decay[x]).astype(BF16)
            gl_s[h, c] = jnp.broadcast_to(jnp.exp(g_last), (8, HEAD_W))
        return carry

    lax.fori_loop(0, nh // WY_GROUP, wy_heads, 0)

    heads = range(nh)
    s = [s_ref[h] for h in heads]
    for c in range(n_chunks):
        lo, hi = c * CHUNK, (c + 1) * CHUNK
        ws = [_dot(wq_s[h, c], s[h].astype(BF16)) for h in heads]
        v_new = [(u_s[h, lo:hi, :] - ws[h][:CHUNK]).astype(BF16) for h in heads]
        o = [ws[h][CHUNK:] + _dot(qk_s[h, c], v_new[h]) for h in heads]
        s = [s[h] * gl_s[h, c][:1, :] + _dot(kdt_s[h, c], v_new[h]) for h in heads]
        for h in heads:
            y = o[h] * lax.rsqrt(jnp.mean(o[h] * o[h], axis=-1, keepdims=True) + EPS)
            o_ref[lo:hi, h * HEAD_W:(h + 1) * HEAD_W] = (
                y * nw_ref[...] * _silu(gg_ref[h, lo:hi, :])).astype(o_ref.dtype)
    for h in heads:
        s_ref[h] = s[h]


def gdn_heads(proj, gates, conv_w, a_log, dt_bias, norm_w, tb):
    t = proj.shape[1]
    nh = N_REC_HEADS
    gw = nh * HEAD_W
    n_chunks = tb // CHUNK

    def group(block):
        return pl.BlockSpec((nh, tb, HEAD_W), lambda i: (block, i, 0))

    def whole(a):
        return pl.BlockSpec(a.shape, lambda i: (0,) * a.ndim)

    a_log = a_log.reshape(nh, 1)
    dt_bias = dt_bias.reshape(nh, 1)
    norm_w = norm_w.reshape(1, HEAD_W)
    return pl.pallas_call(
        functools.partial(_gdn_kernel, n_chunks=n_chunks),
        grid=(t // tb,),
        in_specs=[whole(a_log), whole(dt_bias), group(4), group(5), group(6), group(7),
                  pl.BlockSpec((tb, HEAD_W), lambda i: (i, 0)), whole(conv_w), whole(norm_w)],
        out_specs=pl.BlockSpec((tb, gw), lambda i: (i, 0)),
        out_shape=jax.ShapeDtypeStruct((t, gw), BF16),
        scratch_shapes=[pltpu.VMEM((nh, HEAD_W, HEAD_W), F32),
                        pltpu.VMEM((3, 8, gw), F32),
                        pltpu.VMEM((nh, tb, HEAD_W), F32),
                        pltpu.VMEM((nh, tb, HEAD_W), F32),
                        pltpu.VMEM((nh, tb, HEAD_W), F32),
                        pltpu.VMEM((nh, tb), F32),
                        pltpu.VMEM((nh, tb), F32),
                        pltpu.VMEM((nh, tb, HEAD_W), F32),
                        pltpu.VMEM((nh, n_chunks, 2 * CHUNK, HEAD_W), BF16),
                        pltpu.VMEM((nh, n_chunks, HEAD_W, CHUNK), BF16),
                        pltpu.VMEM((nh, n_chunks, CHUNK, CHUNK), BF16),
                        pltpu.VMEM((nh, n_chunks, 8, HEAD_W), F32)],
        compiler_params=_cparams(("arbitrary",)),
        name="gdn_heads",
    )(a_log, dt_bias, proj, proj, proj, proj, gates, conv_w, norm_w)


def _sb_kernel(q_ref, k_ref, v_ref, o_ref, acc_ref, c_ref, *, n_sub):
    blk = SB_BLK
    qb0 = pl.program_id(1) * n_sub
    row = lax.broadcasted_iota(jnp.int32, (2 * blk, 2 * blk), 0)
    col = lax.broadcasted_iota(jnp.int32, (2 * blk, 2 * blk), 1)
    later2 = (row > col).astype(BF16)
    later = later2[:blk, :blk]
    causal = (col < row)[:blk, :blk]

    def tiles(subs, s, masked, watch=None, width=1):
        n = range(len(subs))
        kw = width * blk
        start = [pl.multiple_of((qb0 + i - s - (width - 1)) * blk, blk) for i in subs]
        z = [_dot_nt(q_ref[i * blk:(i + 1) * blk, :], k_ref[pl.ds(start[x], kw), :])
             for x, i in enumerate(subs)]
        sp = [_softplus(z[x]) for x in n]
        log_1m = [jnp.where(causal, -sp[x], 0.0) if masked else -sp[x] for x in n]
        after = [_dot(log_1m[x].astype(BF16), later if width == 1 else later2)
                 for x in n]
        c = [c_ref[i] for i in subs]
        a = [jnp.exp((z[x] - sp[x]) + after[x] + c[x]) for x in n]
        if masked:
            a = [jnp.where(causal, a[x], 0.0) for x in n]
        c_new = [c[x] + after[x][:, :1] + log_1m[x][:, :1] for x in n]
        for x, i in enumerate(subs):
            acc_ref[i] += _dot(a[x].astype(BF16), v_ref[pl.ds(start[x], kw), :])
            c_ref[i] = c_new[x]
        watched = [c_new[x] for x, i in enumerate(subs) if watch is None or i in watch]
        if not watched:
            return jnp.float32(-jnp.inf)
        return jnp.max(functools.reduce(jnp.maximum, watched))

    all_subs = list(range(n_sub))
    acc_ref[...] = jnp.zeros_like(acc_ref)
    c_ref[...] = jnp.zeros_like(c_ref)
    c_max = tiles(all_subs, 0, True)

    @pl.when(qb0 > 0)
    def _():
        s_end, c_left = lax.while_loop(
            lambda carry: jnp.logical_and(carry[0] + (SB_SWEEP - 1) <= qb0, carry[1] > EXP_ZERO_BELOW),
            lambda carry: (carry[0] + SB_SWEEP, tiles(all_subs, carry[0], False, width=SB_SWEEP)),
            (jnp.int32(1), c_max))

        @pl.when(c_left > EXP_ZERO_BELOW)
        def _():
            for i in range(1, n_sub):
                lax.while_loop(
                    lambda carry, i=i: jnp.logical_and(carry[0] <= qb0 + i, carry[1] > EXP_ZERO_BELOW),
                    lambda carry, i=i: (carry[0] + 1, tiles([i], carry[0], False)),
                    (s_end, jnp.max(c_ref[i])))

    def first_block_sweep(s, c_left):
        if s < n_sub:
            @pl.when(c_left > EXP_ZERO_BELOW)
            def _():
                subs = list(range(s, n_sub))
                first_block_sweep(s + 1, tiles(subs, s, False, watch=subs[1:]))

    @pl.when(qb0 == 0)
    def _():
        first_block_sweep(1, c_max)

    for i in range(n_sub):
        o_ref[i * blk:(i + 1) * blk, :] = acc_ref[i].astype(o_ref.dtype)


def sb_attention(qkv, n_heads, n_sub):
    t = qkv.shape[1]
    qb = n_sub * SB_BLK
    return pl.pallas_call(
        functools.partial(_sb_kernel, n_sub=n_sub),
        grid=(n_heads, t // qb),
        in_specs=[pl.BlockSpec((None, qb, HEAD_W), lambda h, i: (h, i, 0)),
                  pl.BlockSpec((None, t, HEAD_W), lambda h, i: (n_heads + h, 0, 0)),
                  pl.BlockSpec((None, t, HEAD_W), lambda h, i: (2 * n_heads + h, 0, 0))],
        out_specs=pl.BlockSpec((qb, HEAD_W), lambda h, i: (i, h)),
        out_shape=jax.ShapeDtypeStruct((t, n_heads * HEAD_W), BF16),
        scratch_shapes=[pltpu.VMEM((n_sub, SB_BLK, HEAD_W), F32),
                        pltpu.VMEM((n_sub, SB_BLK, 1), F32)],
        compiler_params=_cparams(("parallel", "arbitrary")),
        name="sb_attention",
    )(qkv, qkv, qkv)


def _pick(n, candidates):
    for c in candidates:
        if n % c == 0:
            return c
    return n


def kernel(x, mix_norm, a_w_in, a_conv_w, a_a_log, a_dt_bias, a_lb_logits, a_hgrn_norm,
           a_gdn_norm, a_w_out, c_w_qkv, c_w_o, mlp_norm, mlp_w1, mlp_w2, final_norm):
    bsz, t, d = x.shape
    assert bsz == 1
    depth = mix_norm.shape[0]
    nh = N_REC_HEADS
    kw = nh * HEAD_W
    xs = x.reshape(t, d)
    tm = _pick(t, (1024, 512, 256, 128, 64))
    tm_proj = _pick(t, (512, 256, 128, 64))
    tn_in = _pick(kw, (1024, 512, 256, 128))
    tb_hgrn = _pick(t, (2048, 1024, 512, 256, 128, 64))
    tb_gdn = _pick(t, (256, 128, 64))
    sb_heads = d // HEAD_W
    sb_sub = _pick(t // SB_BLK, (8, 4, 2, 1))

    lb_all = jnp.cumsum(jax.nn.softmax(a_lb_logits.astype(F32), axis=0), axis=0)
    for layer in range(depth):
        j = layer // 2
        if layer % 2 == 0:
            w_in = a_w_in[j]
            n_gate = 2 * nh
            w_in_b = w_in.astype(BF16)
            w_gate = jnp.pad(w_in_b[:, 7 * kw:7 * kw + n_gate], ((0, 0), (0, HEAD_W - n_gate)))
            w_parts = [(w_in_b, 7 * kw // tn_in), (w_in_b[:, 7 * kw + n_gate:], kw // tn_in)]
            proj, gates = norm_matmul(xs, mix_norm[layer], w_parts, jnp.ones((8 * kw,), F32), F32,
                                      tm, tn_in, w_extra=w_gate)
            o_a = hgrn_heads(proj, lb_all[j], a_hgrn_norm[j], tb_hgrn)
            o_b = gdn_heads(proj, gates, a_conv_w[j], a_a_log[j], a_dt_bias[j], a_gdn_norm[j], tb_gdn)
            w_out = a_w_out[j].astype(BF16)
            xs = proj_residual([o_a, o_b], [w_out[:kw], w_out[kw:]], xs, tm_proj, d)
        else:
            scale = jnp.concatenate([jnp.full((d,), HEAD_W ** -0.5, F32), jnp.ones((2 * d,), F32)])
            qkv, = norm_matmul(xs, mix_norm[layer], [(c_w_qkv[j].astype(BF16), 3 * d // tn_in)], scale, BF16,
                               tm, tn_in)
            o_c = sb_attention(qkv, sb_heads, sb_sub)
            xs = proj_residual([o_c], [c_w_o[j].astype(BF16)], xs, tm_proj, d)
        last = layer == depth - 1
        xs = mlp_residual(xs, mlp_norm[layer], mlp_w1[layer].astype(BF16), mlp_w2[layer].astype(BF16),
                          final_norm, last, tm, _pick(mlp_w1.shape[2], (512,)))
    return xs.reshape(bsz, t, d)
```

```python
import functools

import jax
import jax.numpy as jnp
from jax import lax
from jax.experimental import pallas as pl
from jax.experimental.pallas import tpu as pltpu

F32 = jnp.float32
BF16 = jnp.bfloat16
EPS = 1e-6
HEAD_W = 128
CHUNK = 64
SUB = 16
CONV_W = 4
SB_BLK = 128
SB_SWEEP = 2
N_REC_HEADS = 8
WY_GROUP = 8
VMEM_LIMIT = 56 * 1024 * 1024
HI = lax.Precision.HIGHEST
EXP_ZERO_BELOW = -104.0


def _cparams(sem):
    return pltpu.CompilerParams(dimension_semantics=sem, vmem_limit_bytes=VMEM_LIMIT)


def _dot(a, b, precision=None):
    return jnp.dot(a, b, preferred_element_type=F32, precision=precision)


def _dot_nt(a, b, precision=None):
    return lax.dot_general(a, b, (((1,), (1,)), ((), ())),
                           preferred_element_type=F32, precision=precision)


def _rms(x, gain):
    return x * lax.rsqrt(jnp.mean(x * x, axis=-1, keepdims=True) + EPS) * gain


def _sigmoid(x):
    return 0.5 * jnp.tanh(0.5 * x) + 0.5


def _silu(x):
    return x * _sigmoid(x)


def _softplus(x):
    return jnp.maximum(x, 0.0) + jnp.log(1.0 + jnp.exp(-jnp.abs(x)))


def _norm_matmul_kernel(*refs, parts, has_extra):
    n_parts = len(parts)
    x_ref, g_ref = refs[:2]
    w_refs = refs[2:2 + n_parts]
    rest = refs[2 + n_parts:]
    if has_extra:
        cs_ref, we_ref, o_ref, oe_ref, h_ref = rest
    else:
        cs_ref, o_ref, h_ref = rest
    j = pl.program_id(1)

    @pl.when(j == 0)
    def _():
        h_ref[...] = _rms(x_ref[...], g_ref[...]).astype(BF16)
        if has_extra:
            oe_ref[...] = _dot(h_ref[...], we_ref[...])

    def emit(w_ref):
        acc = _dot(h_ref[...], w_ref[...]) * cs_ref[...]
        for c in range(o_ref.shape[0]):
            o_ref[c] = acc[:, c * HEAD_W:(c + 1) * HEAD_W].astype(o_ref.dtype)

    if n_parts == 1:
        emit(w_refs[0])
    else:
        for w_ref, (lo, hi) in zip(w_refs, parts):
            pl.when(jnp.logical_and(j >= lo, j < hi))(functools.partial(emit, w_ref))


def norm_matmul(x, gain, w_parts, col_scale, out_dtype, tm, tn, w_extra=None):
    m, k = x.shape
    has_extra = w_extra is not None
    parts, lo = [], 0
    for _, n_tiles in w_parts:
        parts.append((lo, lo + n_tiles))
        lo += n_tiles
    n = lo * tn
    in_specs = [pl.BlockSpec((tm, k), lambda i, j: (i, 0)),
                pl.BlockSpec((1, k), lambda i, j: (0, 0))]
    for lo, hi in parts:
        in_specs.append(pl.BlockSpec((k, tn), lambda i, j, lo=lo, hi=hi: (0, jnp.clip(j - lo, 0, hi - lo - 1))))
    in_specs.append(pl.BlockSpec((1, tn), lambda i, j: (0, j)))
    out_specs = [pl.BlockSpec((tn // HEAD_W, tm, HEAD_W), lambda i, j: (j, i, 0))]
    out_shape = [jax.ShapeDtypeStruct((n // HEAD_W, m, HEAD_W), out_dtype)]
    args = [x, gain.reshape(1, k)] + [w for w, _ in w_parts] + [col_scale.reshape(1, n)]
    if has_extra:
        in_specs.append(pl.BlockSpec((k, HEAD_W), lambda i, j: (0, 0)))
        out_specs.append(pl.BlockSpec((tm, HEAD_W), lambda i, j: (i, 0)))
        out_shape.append(jax.ShapeDtypeStruct((m, HEAD_W), F32))
        args.append(w_extra)
    return pl.pallas_call(
        functools.partial(_norm_matmul_kernel, parts=tuple(parts), has_extra=has_extra),
        grid=(m // tm, n // tn),
        in_specs=in_specs,
        out_specs=out_specs,
        out_shape=out_shape,
        scratch_shapes=[pltpu.VMEM((tm, k), BF16)],
        compiler_params=_cparams(("parallel", "arbitrary")),
        name="norm_matmul",
    )(*args)


def _proj_residual_kernel(*refs, n_lhs):
    lhs = refs[:n_lhs]
    ws = refs[n_lhs:2 * n_lhs]
    x_ref, o_ref = refs[2 * n_lhs], refs[2 * n_lhs + 1]
    acc = x_ref[...]
    for a_ref, w_ref in zip(lhs, ws):
        acc = acc + _dot(a_ref[...], w_ref[...])
    o_ref[...] = acc


def proj_residual(lhs_list, w_list, x, tm, tn):
    m, n = x.shape
    n_lhs = len(lhs_list)
    in_specs = [pl.BlockSpec((tm, a.shape[1]), lambda i, j: (i, 0)) for a in lhs_list]
    in_specs += [pl.BlockSpec((w.shape[0], tn), lambda i, j: (0, j)) for w in w_list]
    in_specs += [pl.BlockSpec((tm, tn), lambda i, j: (i, j))]
    return pl.pallas_call(
        functools.partial(_proj_residual_kernel, n_lhs=n_lhs),
        grid=(m // tm, n // tn),
        in_specs=in_specs,
        out_specs=pl.BlockSpec((tm, tn), lambda i, j: (i, j)),
        out_shape=jax.ShapeDtypeStruct((m, n), F32),
        compiler_params=_cparams(("parallel", "arbitrary")),
        name="proj_residual",
    )(*lhs_list, *w_list, x)


def _mlp_kernel(x_ref, g_ref, w1_ref, w2_ref, fg_ref, o_ref, h_ref, *, final_norm):
    j = pl.program_id(1)

    @pl.when(j == 0)
    def _():
        x = x_ref[...]
        h_ref[...] = _rms(x, g_ref[...]).astype(BF16)
        o_ref[...] = x

    a = _dot(h_ref[...], w1_ref[...])
    a = jnp.square(jnp.maximum(a, 0.0)).astype(BF16)
    o_ref[...] += _dot(a, w2_ref[...])

    if final_norm:
        @pl.when(j == pl.num_programs(1) - 1)
        def _():
            o_ref[...] = _rms(o_ref[...], fg_ref[...])


def mlp_residual(x, gain, w1, w2, layer, final_gain, final_norm, tm, tf):
    m, d = x.shape
    ff = w1.shape[2]
    return pl.pallas_call(
        functools.partial(_mlp_kernel, final_norm=final_norm),
        grid=(m // tm, ff // tf),
        in_specs=[
            pl.BlockSpec((tm, d), lambda i, j: (i, 0)),
            pl.BlockSpec((1, d), lambda i, j: (0, 0)),
            pl.BlockSpec((None, d, tf), lambda i, j: (layer, 0, j)),
            pl.BlockSpec((None, tf, d), lambda i, j: (layer, j, 0)),
            pl.BlockSpec((1, d), lambda i, j: (0, 0)),
        ],
        out_specs=pl.BlockSpec((tm, d), lambda i, j: (i, 0)),
        out_shape=jax.ShapeDtypeStruct((m, d), F32),
        scratch_shapes=[pltpu.VMEM((tm, d), BF16)],
        compiler_params=_cparams(("parallel", "arbitrary")),
        name="mlp_residual",
    )(x, gain.reshape(1, d), w1, w2, final_gain.reshape(1, d))


def _hgrn_kernel(hq_ref, hf_ref, hi_ref, hg_ref, lb_ref, nw_ref, o_ref, st_ref, *, n_chunks):
    @pl.when(pl.program_id(1) == 0)
    def _():
        st_ref[...] = jnp.zeros_like(st_ref)

    chunks = range(n_chunks)
    n_sub = CHUNK // SUB
    lb = lb_ref[...]
    row = lax.broadcasted_iota(jnp.int32, (CHUNK, CHUNK), 0)
    col = lax.broadcasted_iota(jnp.int32, (CHUNK, CHUNK), 1)
    tril = (col <= row).astype(F32)

    hf = hf_ref[...]
    log_f = jnp.log(lb + (1.0 - lb) * _sigmoid(hf))
    k_all = (1.0 - lb) * _sigmoid(-hf)
    q_all = _silu(hq_ref[...])
    vb_all = hi_ref[...].astype(BF16)

    rows = [slice(c * CHUNK, (c + 1) * CHUNK) for c in chunks]
    q = [q_all[r] for r in rows]
    k = [k_all[r] for r in rows]
    vb = [vb_all[r] for r in rows]
    b = [_dot(tril, log_f[r], precision=HI) for r in rows]
    o_intra = [[] for _ in chunks]
    for i in range(n_sub):
        lo, hi = i * SUB, (i + 1) * SUB
        srow = lax.broadcasted_iota(jnp.int32, (SUB, hi), 0)
        scol = lax.broadcasted_iota(jnp.int32, (SUB, hi), 1)
        causal = scol <= srow + lo
        b_ref = [b[c][lo:lo + 1, :] for c in chunks]
        qs = [(q[c][lo:hi] * jnp.exp(b[c][lo:hi] - b_ref[c])).astype(BF16) for c in chunks]
        ks = [(k[c][:hi] * jnp.exp(b_ref[c] - b[c][:hi])).astype(BF16) for c in chunks]
        att = [jnp.where(causal, _dot_nt(qs[c], ks[c]), 0.0).astype(BF16) for c in chunks]
        for c in chunks:
            o_intra[c].append(_dot(att[c], vb[c][:hi]))
    b_last = [b[c][CHUNK - 1:CHUNK, :] for c in chunks]
    qd = [(q[c] * jnp.exp(b[c])).astype(BF16) for c in chunks]
    kd = [(k[c] * jnp.exp(b_last[c] - b[c])).astype(BF16) for c in chunks]
    p = [_dot(vb[c].T, kd[c]) for c in chunks]
    d = [jnp.exp(b_last[c]) for c in chunks]
    gate = nw_ref[...] * _silu(hg_ref[...])

    st = st_ref[...]
    for c in chunks:
        o = jnp.concatenate(o_intra[c], axis=0) + _dot_nt(qd[c], st.astype(BF16))
        st = st * d[c] + p[c]
        y = o * lax.rsqrt(jnp.mean(o * o, axis=-1, keepdims=True) + EPS)
        o_ref[rows[c], :] = (y * gate[rows[c]]).astype(o_ref.dtype)
    st_ref[...] = st


def hgrn_heads(proj, lb, norm_w, tb):
    t = proj.shape[1]
    nh = N_REC_HEADS

    def col(block0):
        return pl.BlockSpec((None, tb, HEAD_W), lambda h, i: (block0 + h, i, 0))

    return pl.pallas_call(
        functools.partial(_hgrn_kernel, n_chunks=tb // CHUNK),
        grid=(nh, t // tb),
        in_specs=[col(0), col(nh), col(2 * nh), col(3 * nh),
                  pl.BlockSpec((None, 1, HEAD_W), lambda h, i: (h, 0, 0)),
                  pl.BlockSpec((1, HEAD_W), lambda h, i: (0, 0))],
        out_specs=pl.BlockSpec((tb, HEAD_W), lambda h, i: (i, h)),
        out_shape=jax.ShapeDtypeStruct((t, nh * HEAD_W), BF16),
        scratch_shapes=[pltpu.VMEM((HEAD_W, HEAD_W), F32)],
        compiler_params=_cparams(("parallel", "arbitrary")),
        name="hgrn_heads",
    )(proj, proj, proj, proj, lb.reshape(nh, 1, HEAD_W), norm_w.reshape(1, HEAD_W))


def _gdn_kernel(alog_ref, dtb_ref, gq_ref, gk_ref, gv_ref, gg_ref, gab_ref, cw_ref, nw_ref, o_ref,
                s_ref, tail_ref, q_s, k_s, v_s, g_s, beta_s, u_s, wq_s, kdt_s, qk_s, gl_s, *, n_chunks):
    tb = n_chunks * CHUNK
    nh = N_REC_HEADS

    @pl.when(pl.program_id(0) == 0)
    def _():
        s_ref[...] = jnp.zeros_like(s_ref)
        tail_ref[...] = jnp.zeros_like(tail_ref)

    def conv_silu(u_ref, slot, h):
        cols = slice(h * HEAD_W, (h + 1) * HEAD_W)
        u = u_ref[h]
        ext = jnp.concatenate([tail_ref[slot, :, cols], u], axis=0)
        w = cw_ref[:, slot * nh * HEAD_W + h * HEAD_W:slot * nh * HEAD_W + (h + 1) * HEAD_W]
        acc = u * w[CONV_W - 1:CONV_W, :]
        for j in range(CONV_W - 1):
            acc = acc + pltpu.roll(ext, CONV_W - 1 - j, axis=0)[8:, :] * w[j:j + 1, :]
        tail_ref[slot, :, cols] = u[tb - 8:, :]
        return _silu(acc)

    for h in range(nh):
        qh = conv_silu(gq_ref, 0, h)
        kh = conv_silu(gk_ref, 1, h)
        q_s[h] = qh * lax.rsqrt(jnp.sum(qh * qh, axis=-1, keepdims=True) + EPS) * (HEAD_W ** -0.5)
        k_s[h] = kh * lax.rsqrt(jnp.sum(kh * kh, axis=-1, keepdims=True) + EPS)
        v_s[h] = conv_silu(gv_ref, 2, h)
    gab = gab_ref[...].T
    g = -jnp.exp(alog_ref[...]) * _softplus(gab[:nh, :] + dtb_ref[...])
    brow = lax.broadcasted_iota(jnp.int32, (tb, tb), 0)
    bcol = lax.broadcasted_iota(jnp.int32, (tb, tb), 1)
    same_chunk_upto = jnp.logical_and(brow <= bcol, brow // CHUNK == bcol // CHUNK).astype(F32)
    g_s[...] = _dot(g, same_chunk_upto, precision=HI)
    beta_s[...] = _sigmoid(gab[nh:2 * nh, :])

    row = lax.broadcasted_iota(jnp.int32, (CHUNK, CHUNK), 0)
    col = lax.broadcasted_iota(jnp.int32, (CHUNK, CHUNK), 1)
    tril = col <= row
    strict = col < row
    eye = col == row
    eye_f = eye.astype(F32)

    def wy_heads(hg, carry):
        items = [(hg * WY_GROUP + dh, c) for dh in range(WY_GROUP) for c in range(n_chunks)]
        n = range(len(items))
        rows = [slice(c * CHUNK, (c + 1) * CHUNK) for _, c in items]
        q = [q_s[h, rows[x], :] for x, (h, _) in enumerate(items)]
        k = [k_s[h, rows[x], :] for x, (h, _) in enumerate(items)]
        gc_all = [g_s[pl.ds(hg * WY_GROUP + dh, 1), :] for dh in range(WY_GROUP)]
        beta_all = [beta_s[pl.ds(hg * WY_GROUP + dh, 1), :] for dh in range(WY_GROUP)]
        gc_row = [jnp.broadcast_to(gc_all[x // n_chunks][:, rows[x]], (CHUNK, CHUNK)) for x in n]
        beta_b = [jnp.broadcast_to(beta_all[x // n_chunks][:, rows[x]], (CHUNK, CHUNK)) for x in n]
        gc_col = [jnp.sum(jnp.where(eye, gc_row[x], 0.0), axis=-1, keepdims=True) for x in n]
        beta_col = [jnp.sum(jnp.where(eye, beta_b[x], 0.0), axis=-1, keepdims=True) for x in n]
        decay = [jnp.where(tril, jnp.exp(jnp.minimum(gc_col[x] - gc_row[x], 0.0)), 0.0) for x in n]
        kb = [k[x] * beta_col[x] for x in n]
        prod = [_dot_nt(jnp.concatenate([kb[x], q[x]], axis=0).astype(BF16), k[x].astype(BF16))
                for x in n]
        n1 = [jnp.where(strict, -prod[x][:CHUNK] * decay[x], 0.0) for x in n]
        inv = [eye_f + n1[x] for x in n]
        npow = [n1[x].astype(BF16) for x in n]
        for _ in range(5):
            npow = [_dot(npow[x], npow[x]).astype(BF16) for x in n]
            inv = [inv[x] + _dot(inv[x].astype(BF16), npow[x]) for x in n]
        e_gc = [jnp.exp(gc_col[x]) for x in n]
        rhs = [jnp.concatenate([v_s[h, rows[x], :] * beta_col[x], kb[x] * e_gc[x]], axis=1)
               for x, (h, _) in enumerate(items)]
        sol = [_dot(inv[x].astype(BF16), rhs[x].astype(BF16)) for x in n]
        for x, (h, c) in enumerate(items):
            g_last = gc_col[x][CHUNK - 1:CHUNK, :]
            u_s[h, rows[x], :] = sol[x][:, :HEAD_W]
            wq_s[h, c, :CHUNK, :] = sol[x][:, HEAD_W:].astype(BF16)
            wq_s[h, c, CHUNK:, :] = (q[x] * e_gc[x]).astype(BF16)
            kdt_s[h, c] = (k[x] * jnp.exp(g_last - gc_col[x])).T.astype(BF16)
            qk_s[h, c] = (prod[x][CHUNK:] * decay[x]).astype(BF16)
            gl_s[h, c] = jnp.broadcast_to(jnp.exp(g_last), (8, HEAD_W))
        return carry

    lax.fori_loop(0, nh // WY_GROUP, wy_heads, 0)

    heads = range(nh)
    s = [s_ref[h] for h in heads]
    for c in range(n_chunks):
        lo, hi = c * CHUNK, (c + 1) * CHUNK
        ws = [_dot(wq_s[h, c], s[h].astype(BF16)) for h in heads]
        v_new = [(u_s[h, lo:hi, :] - ws[h][:CHUNK]).astype(BF16) for h in heads]
        o = [ws[h][CHUNK:] + _dot(qk_s[h, c], v_new[h]) for h in heads]
        s = [s[h] * gl_s[h, c][:1, :] + _dot(kdt_s[h, c], v_new[h]) for h in heads]
        for h in heads:
            y = o[h] * lax.rsqrt(jnp.mean(o[h] * o[h], axis=-1, keepdims=True) + EPS)
            o_ref[lo:hi, h * HEAD_W:(h + 1) * HEAD_W] = (
                y * nw_ref[...] * _silu(gg_ref[h, lo:hi, :])).astype(o_ref.dtype)
    for h in heads:
        s_ref[h] = s[h]


def gdn_heads(proj, gates, conv_w, a_log, dt_bias, norm_w, tb):
    t = proj.shape[1]
    nh = N_REC_HEADS
    gw = nh * HEAD_W
    n_chunks = tb // CHUNK

    def group(block):
        return pl.BlockSpec((nh, tb, HEAD_W), lambda i: (block, i, 0))

    def whole(a):
        return pl.BlockSpec(a.shape, lambda i: (0,) * a.ndim)

    a_log = a_log.reshape(nh, 1)
    dt_bias = dt_bias.reshape(nh, 1)
    norm_w = norm_w.reshape(1, HEAD_W)
    return pl.pallas_call(
        functools.partial(_gdn_kernel, n_chunks=n_chunks),
        grid=(t // tb,),
        in_specs=[whole(a_log), whole(dt_bias), group(4), group(5), group(6), group(7),
                  pl.BlockSpec((tb, HEAD_W), lambda i: (i, 0)), whole(conv_w), whole(norm_w)],
        out_specs=pl.BlockSpec((tb, gw), lambda i: (i, 0)),
        out_shape=jax.ShapeDtypeStruct((t, gw), BF16),
        scratch_shapes=[pltpu.VMEM((nh, HEAD_W, HEAD_W), F32),
                        pltpu.VMEM((3, 8, gw), F32),
                        pltpu.VMEM((nh, tb, HEAD_W), F32),
                        pltpu.VMEM((nh, tb, HEAD_W), F32),
                        pltpu.VMEM((nh, tb, HEAD_W), F32),
                        pltpu.VMEM((nh, tb), F32),
                        pltpu.VMEM((nh, tb), F32),
                        pltpu.VMEM((nh, tb, HEAD_W), F32),
                        pltpu.VMEM((nh, n_chunks, 2 * CHUNK, HEAD_W), BF16),
                        pltpu.VMEM((nh, n_chunks, HEAD_W, CHUNK), BF16),
                        pltpu.VMEM((nh, n_chunks, CHUNK, CHUNK), BF16),
                        pltpu.VMEM((nh, n_chunks, 8, HEAD_W), F32)],
        compiler_params=_cparams(("arbitrary",)),
        name="gdn_heads",
    )(a_log, dt_bias, proj, proj, proj, proj, gates, conv_w, norm_w)


def _sb_kernel(q_ref, k_ref, v_ref, o_ref, acc_ref, c_ref, *, n_sub):
    blk = SB_BLK
    qb0 = pl.program_id(1) * n_sub
    row = lax.broadcasted_iota(jnp.int32, (2 * blk, 2 * blk), 0)
    col = lax.broadcasted_iota(jnp.int32, (2 * blk, 2 * blk), 1)
    later2 = (row > col).astype(BF16)
    later = later2[:blk, :blk]
    causal = (col < row)[:blk, :blk]

    def tiles(subs, s, masked, watch=None, width=1):
        n = range(len(subs))
        kw = width * blk
        start = [pl.multiple_of((qb0 + i - s - (width - 1)) * blk, blk) for i in subs]
        z = [_dot_nt(q_ref[i * blk:(i + 1) * blk, :], k_ref[pl.ds(start[x], kw), :])
             for x, i in enumerate(subs)]
        sp = [_softplus(z[x]) for x in n]
        log_1m = [jnp.where(causal, -sp[x], 0.0) if masked else -sp[x] for x in n]
        after = [_dot(log_1m[x].astype(BF16), later if width == 1 else later2)
                 for x in n]
        c = [c_ref[i] for i in subs]
        a = [jnp.exp((z[x] - sp[x]) + after[x] + c[x]) for x in n]
        if masked:
            a = [jnp.where(causal, a[x], 0.0) for x in n]
        c_new = [c[x] + after[x][:, :1] + log_1m[x][:, :1] for x in n]
        for x, i in enumerate(subs):
            acc_ref[i] += _dot(a[x].astype(BF16), v_ref[pl.ds(start[x], kw), :])
            c_ref[i] = c_new[x]
        watched = [c_new[x] for x, i in enumerate(subs) if watch is None or i in watch]
        if not watched:
            return jnp.float32(-jnp.inf)
        return jnp.max(functools.reduce(jnp.maximum, watched))

    all_subs = list(range(n_sub))
    acc_ref[...] = jnp.zeros_like(acc_ref)
    c_ref[...] = jnp.zeros_like(c_ref)
    c_max = tiles(all_subs, 0, True)

    @pl.when(qb0 > 0)
    def _():
        s_end, c_left = lax.while_loop(
            lambda carry: jnp.logical_and(carry[0] + (SB_SWEEP - 1) <= qb0, carry[1] > EXP_ZERO_BELOW),
            lambda carry: (carry[0] + SB_SWEEP, tiles(all_subs, carry[0], False, width=SB_SWEEP)),
            (jnp.int32(1), c_max))

        @pl.when(c_left > EXP_ZERO_BELOW)
        def _():
            for i in range(1, n_sub):
                lax.while_loop(
                    lambda carry, i=i: jnp.logical_and(carry[0] <= qb0 + i, carry[1] > EXP_ZERO_BELOW),
                    lambda carry, i=i: (carry[0] + 1, tiles([i], carry[0], False)),
                    (s_end, jnp.max(c_ref[i])))

    def first_block_sweep(s, c_left):
        if s < n_sub:
            @pl.when(c_left > EXP_ZERO_BELOW)
            def _():
                subs = list(range(s, n_sub))
                first_block_sweep(s + 1, tiles(subs, s, False, watch=subs[1:]))

    @pl.when(qb0 == 0)
    def _():
        first_block_sweep(1, c_max)

    for i in range(n_sub):
        o_ref[i * blk:(i + 1) * blk, :] = acc_ref[i].astype(o_ref.dtype)


def sb_attention(qkv, n_heads, n_sub):
    t = qkv.shape[1]
    qb = n_sub * SB_BLK
    return pl.pallas_call(
        functools.partial(_sb_kernel, n_sub=n_sub),
        grid=(n_heads, t // qb),
        in_specs=[pl.BlockSpec((None, qb, HEAD_W), lambda h, i: (h, i, 0)),
                  pl.BlockSpec((None, t, HEAD_W), lambda h, i: (n_heads + h, 0, 0)),
                  pl.BlockSpec((None, t, HEAD_W), lambda h, i: (2 * n_heads + h, 0, 0))],
        out_specs=pl.BlockSpec((qb, HEAD_W), lambda h, i: (i, h)),
        out_shape=jax.ShapeDtypeStruct((t, n_heads * HEAD_W), BF16),
        scratch_shapes=[pltpu.VMEM((n_sub, SB_BLK, HEAD_W), F32),
                        pltpu.VMEM((n_sub, SB_BLK, 1), F32)],
        compiler_params=_cparams(("parallel", "arbitrary")),
        name="sb_attention",
    )(qkv, qkv, qkv)


def _pick(n, candidates):
    for c in candidates:
        if n % c == 0:
            return c
    return n


def kernel(x, mix_norm, a_w_in, a_conv_w, a_a_log, a_dt_bias, a_lb_logits, a_hgrn_norm,
           a_gdn_norm, a_w_out, c_w_qkv, c_w_o, mlp_norm, mlp_w1, mlp_w2, final_norm):
    bsz, t, d = x.shape
    assert bsz == 1
    depth = mix_norm.shape[0]
    nh = N_REC_HEADS
    kw = nh * HEAD_W
    xs = x.reshape(t, d)
    tm = _pick(t, (1024, 512, 256, 128, 64))
    tm_proj = _pick(t, (512, 256, 128, 64))
    tn_in = _pick(kw, (1024, 512, 256, 128))
    tb_hgrn = _pick(t, (2048, 1024, 512, 256, 128, 64))
    tb_gdn = _pick(t, (256, 128, 64))
    sb_heads = d // HEAD_W
    sb_sub = _pick(t // SB_BLK, (8, 4, 2, 1))

    lb_all = jnp.cumsum(jax.nn.softmax(a_lb_logits.astype(F32), axis=0), axis=0)
    mlp_w1_b, mlp_w2_b = mlp_w1.astype(BF16), mlp_w2.astype(BF16)
    for layer in range(depth):
        j = layer // 2
        if layer % 2 == 0:
            w_in = a_w_in[j]
            n_gate = 2 * nh
            w_in_b = w_in.astype(BF16)
            w_gate = jnp.pad(w_in_b[:, 7 * kw:7 * kw + n_gate], ((0, 0), (0, HEAD_W - n_gate)))
            w_parts = [(w_in_b, 7 * kw // tn_in), (w_in_b[:, 7 * kw + n_gate:], kw // tn_in)]
            proj, gates = norm_matmul(xs, mix_norm[layer], w_parts, jnp.ones((8 * kw,), F32), F32,
                                      tm, tn_in, w_extra=w_gate)
            o_a = hgrn_heads(proj, lb_all[j], a_hgrn_norm[j], tb_hgrn)
            o_b = gdn_heads(proj, gates, a_conv_w[j], a_a_log[j], a_dt_bias[j], a_gdn_norm[j], tb_gdn)
            w_out = a_w_out[j].astype(BF16)
            xs = proj_residual([o_a, o_b], [w_out[:kw], w_out[kw:]], xs, tm_proj, d)
        else:
            scale = jnp.concatenate([jnp.full((d,), HEAD_W ** -0.5, F32), jnp.ones((2 * d,), F32)])
            qkv, = norm_matmul(xs, mix_norm[layer], [(c_w_qkv[j].astype(BF16), 3 * d // tn_in)], scale, BF16,
                               tm, tn_in)
            o_c = sb_attention(qkv, sb_heads, sb_sub)
            xs = proj_residual([o_c], [c_w_o[j].astype(BF16)], xs, tm_proj, d)
        last = layer == depth - 1
        xs = mlp_residual(xs, mlp_norm[layer], mlp_w1_b, mlp_w2_b, layer,
                          final_norm, last, tm, _pick(mlp_w1.shape[2], (512,)))
    return xs.reshape(bsz, t, d)
```

```python
import functools

import jax
import jax.numpy as jnp
from jax import lax
from jax.experimental import pallas as pl
from jax.experimental.pallas import tpu as pltpu

F32 = jnp.float32
BF16 = jnp.bfloat16
EPS = 1e-6
HEAD_W = 128
CHUNK = 64
SUB = 16
CONV_W = 4
SB_BLK = 128
SB_SWEEP = 2
SB_FIRST_UNROLL = 2
N_REC_HEADS = 8
WY_GROUP = 8
VMEM_LIMIT = 56 * 1024 * 1024
HI = lax.Precision.HIGHEST
EXP_ZERO_BELOW = -104.0


def _cparams(sem):
    return pltpu.CompilerParams(dimension_semantics=sem, vmem_limit_bytes=VMEM_LIMIT)


def _dot(a, b, precision=None):
    return jnp.dot(a, b, preferred_element_type=F32, precision=precision)


def _dot_nt(a, b, precision=None):
    return lax.dot_general(a, b, (((1,), (1,)), ((), ())),
                           preferred_element_type=F32, precision=precision)


def _rms(x, gain):
    return x * lax.rsqrt(jnp.mean(x * x, axis=-1, keepdims=True) + EPS) * gain


def _sigmoid(x):
    return 0.5 * jnp.tanh(0.5 * x) + 0.5


def _silu(x):
    return x * _sigmoid(x)


def _softplus(x):
    return jnp.maximum(x, 0.0) + jnp.log(1.0 + jnp.exp(-jnp.abs(x)))


def _norm_matmul_kernel(*refs, parts, has_extra):
    n_parts = len(parts)
    x_ref, g_ref = refs[:2]
    w_refs = refs[2:2 + n_parts]
    rest = refs[2 + n_parts:]
    if has_extra:
        cs_ref, we_ref, o_ref, oe_ref, h_ref = rest
    else:
        cs_ref, o_ref, h_ref = rest
    j = pl.program_id(1)

    @pl.when(j == 0)
    def _():
        h_ref[...] = _rms(x_ref[...], g_ref[...]).astype(BF16)
        if has_extra:
            oe_ref[...] = _dot(h_ref[...], we_ref[...])

    def emit(w_ref):
        acc = _dot(h_ref[...], w_ref[...]) * cs_ref[...]
        for c in range(o_ref.shape[0]):
            o_ref[c] = acc[:, c * HEAD_W:(c + 1) * HEAD_W].astype(o_ref.dtype)

    if n_parts == 1:
        emit(w_refs[0])
    else:
        for w_ref, (lo, hi) in zip(w_refs, parts):
            pl.when(jnp.logical_and(j >= lo, j < hi))(functools.partial(emit, w_ref))


def norm_matmul(x, gain, w_parts, col_scale, out_dtype, tm, tn, w_extra=None):
    m, k = x.shape
    has_extra = w_extra is not None
    parts, lo = [], 0
    for _, n_tiles in w_parts:
        parts.append((lo, lo + n_tiles))
        lo += n_tiles
    n = lo * tn
    in_specs = [pl.BlockSpec((tm, k), lambda i, j: (i, 0)),
                pl.BlockSpec((1, k), lambda i, j: (0, 0))]
    for lo, hi in parts:
        in_specs.append(pl.BlockSpec((k, tn), lambda i, j, lo=lo, hi=hi: (0, jnp.clip(j - lo, 0, hi - lo - 1))))
    in_specs.append(pl.BlockSpec((1, tn), lambda i, j: (0, j)))
    out_specs = [pl.BlockSpec((tn // HEAD_W, tm, HEAD_W), lambda i, j: (j, i, 0))]
    out_shape = [jax.ShapeDtypeStruct((n // HEAD_W, m, HEAD_W), out_dtype)]
    args = [x, gain.reshape(1, k)] + [w for w, _ in w_parts] + [col_scale.reshape(1, n)]
    if has_extra:
        in_specs.append(pl.BlockSpec((k, HEAD_W), lambda i, j: (0, 0)))
        out_specs.append(pl.BlockSpec((tm, HEAD_W), lambda i, j: (i, 0)))
        out_shape.append(jax.ShapeDtypeStruct((m, HEAD_W), F32))
        args.append(w_extra)
    return pl.pallas_call(
        functools.partial(_norm_matmul_kernel, parts=tuple(parts), has_extra=has_extra),
        grid=(m // tm, n // tn),
        in_specs=in_specs,
        out_specs=out_specs,
        out_shape=out_shape,
        scratch_shapes=[pltpu.VMEM((tm, k), BF16)],
        compiler_params=_cparams(("parallel", "arbitrary")),
        name="norm_matmul",
    )(*args)


def _proj_residual_kernel(*refs, n_lhs):
    lhs = refs[:n_lhs]
    ws = refs[n_lhs:2 * n_lhs]
    x_ref, o_ref = refs[2 * n_lhs], refs[2 * n_lhs + 1]
    acc = x_ref[...]
    for a_ref, w_ref in zip(lhs, ws):
        acc = acc + _dot(a_ref[...], w_ref[...])
    o_ref[...] = acc


def proj_residual(lhs_list, w_list, x, tm, tn):
    m, n = x.shape
    n_lhs = len(lhs_list)
    in_specs = [pl.BlockSpec((tm, a.shape[1]), lambda i, j: (i, 0)) for a in lhs_list]
    in_specs += [pl.BlockSpec((w.shape[0], tn), lambda i, j: (0, j)) for w in w_list]
    in_specs += [pl.BlockSpec((tm, tn), lambda i, j: (i, j))]
    return pl.pallas_call(
        functools.partial(_proj_residual_kernel, n_lhs=n_lhs),
        grid=(m // tm, n // tn),
        in_specs=in_specs,
        out_specs=pl.BlockSpec((tm, tn), lambda i, j: (i, j)),
        out_shape=jax.ShapeDtypeStruct((m, n), F32),
        compiler_params=_cparams(("parallel", "arbitrary")),
        name="proj_residual",
    )(*lhs_list, *w_list, x)


def _mlp_kernel(x_ref, g_ref, w1_ref, w2_ref, fg_ref, o_ref, h_ref, *, final_norm):
    j = pl.program_id(1)

    @pl.when(j == 0)
    def _():
        x = x_ref[...]
        h_ref[...] = _rms(x, g_ref[...]).astype(BF16)
        o_ref[...] = x

    a = _dot(h_ref[...], w1_ref[...])
    a = jnp.square(jnp.maximum(a, 0.0)).astype(BF16)
    o_ref[...] += _dot(a, w2_ref[...])

    if final_norm:
        @pl.when(j == pl.num_programs(1) - 1)
        def _():
            o_ref[...] = _rms(o_ref[...], fg_ref[...])


def mlp_residual(x, gain, w1, w2, layer, final_gain, final_norm, tm, tf):
    m, d = x.shape
    ff = w1.shape[2]
    return pl.pallas_call(
        functools.partial(_mlp_kernel, final_norm=final_norm),
        grid=(m // tm, ff // tf),
        in_specs=[
            pl.BlockSpec((tm, d), lambda i, j: (i, 0)),
            pl.BlockSpec((1, d), lambda i, j: (0, 0)),
            pl.BlockSpec((None, d, tf), lambda i, j: (layer, 0, j)),
            pl.BlockSpec((None, tf, d), lambda i, j: (layer, j, 0)),
            pl.BlockSpec((1, d), lambda i, j: (0, 0)),
        ],
        out_specs=pl.BlockSpec((tm, d), lambda i, j: (i, 0)),
        out_shape=jax.ShapeDtypeStruct((m, d), F32),
        scratch_shapes=[pltpu.VMEM((tm, d), BF16)],
        compiler_params=_cparams(("parallel", "arbitrary")),
        name="mlp_residual",
    )(x, gain.reshape(1, d), w1, w2, final_gain.reshape(1, d))


def _hgrn_kernel(hq_ref, hf_ref, hi_ref, hg_ref, lb_ref, nw_ref, o_ref, st_ref, *, n_chunks):
    @pl.when(pl.program_id(1) == 0)
    def _():
        st_ref[...] = jnp.zeros_like(st_ref)

    chunks = range(n_chunks)
    n_sub = CHUNK // SUB
    lb = lb_ref[...]
    row = lax.broadcasted_iota(jnp.int32, (CHUNK, CHUNK), 0)
    col = lax.broadcasted_iota(jnp.int32, (CHUNK, CHUNK), 1)
    tril = (col <= row).astype(F32)

    hf = hf_ref[...]
    log_f = jnp.log(lb + (1.0 - lb) * _sigmoid(hf))
    k_all = (1.0 - lb) * _sigmoid(-hf)
    q_all = _silu(hq_ref[...])
    vb_all = hi_ref[...].astype(BF16)

    rows = [slice(c * CHUNK, (c + 1) * CHUNK) for c in chunks]
    q = [q_all[r] for r in rows]
    k = [k_all[r] for r in rows]
    vb = [vb_all[r] for r in rows]
    b = [_dot(tril, log_f[r], precision=HI) for r in rows]
    o_intra = [[] for _ in chunks]
    for i in range(n_sub):
        lo, hi = i * SUB, (i + 1) * SUB
        srow = lax.broadcasted_iota(jnp.int32, (SUB, hi), 0)
        scol = lax.broadcasted_iota(jnp.int32, (SUB, hi), 1)
        causal = scol <= srow + lo
        b_ref = [b[c][lo:lo + 1, :] for c in chunks]
        qs = [(q[c][lo:hi] * jnp.exp(b[c][lo:hi] - b_ref[c])).astype(BF16) for c in chunks]
        ks = [(k[c][:hi] * jnp.exp(b_ref[c] - b[c][:hi])).astype(BF16) for c in chunks]
        att = [jnp.where(causal, _dot_nt(qs[c], ks[c]), 0.0).astype(BF16) for c in chunks]
        for c in chunks:
            o_intra[c].append(_dot(att[c], vb[c][:hi]))
    b_last = [b[c][CHUNK - 1:CHUNK, :] for c in chunks]
    qd = [(q[c] * jnp.exp(b[c])).astype(BF16) for c in chunks]
    kd = [(k[c] * jnp.exp(b_last[c] - b[c])).astype(BF16) for c in chunks]
    p = [_dot(vb[c].T, kd[c]) for c in chunks]
    d = [jnp.exp(b_last[c]) for c in chunks]
    gate = nw_ref[...] * _silu(hg_ref[...])

    st = st_ref[...]
    for c in chunks:
        o = jnp.concatenate(o_intra[c], axis=0) + _dot_nt(qd[c], st.astype(BF16))
        st = st * d[c] + p[c]
        y = o * lax.rsqrt(jnp.mean(o * o, axis=-1, keepdims=True) + EPS)
        o_ref[rows[c], :] = (y * gate[rows[c]]).astype(o_ref.dtype)
    st_ref[...] = st


def hgrn_heads(proj, lb, norm_w, tb):
    t = proj.shape[1]
    nh = N_REC_HEADS

    def col(block0):
        return pl.BlockSpec((None, tb, HEAD_W), lambda h, i: (block0 + h, i, 0))

    return pl.pallas_call(
        functools.partial(_hgrn_kernel, n_chunks=tb // CHUNK),
        grid=(nh, t // tb),
        in_specs=[col(0), col(nh), col(2 * nh), col(3 * nh),
                  pl.BlockSpec((None, 1, HEAD_W), lambda h, i: (h, 0, 0)),
                  pl.BlockSpec((1, HEAD_W), lambda h, i: (0, 0))],
        out_specs=pl.BlockSpec((tb, HEAD_W), lambda h, i: (i, h)),
        out_shape=jax.ShapeDtypeStruct((t, nh * HEAD_W), BF16),
        scratch_shapes=[pltpu.VMEM((HEAD_W, HEAD_W), F32)],
        compiler_params=_cparams(("parallel", "arbitrary")),
        name="hgrn_heads",
    )(proj, proj, proj, proj, lb.reshape(nh, 1, HEAD_W), norm_w.reshape(1, HEAD_W))


def _gdn_kernel(alog_ref, dtb_ref, gq_ref, gk_ref, gv_ref, gg_ref, gab_ref, cw_ref, nw_ref, o_ref,
                s_ref, tail_ref, q_s, k_s, v_s, g_s, beta_s, u_s, wq_s, kdt_s, qk_s, gl_s, *, n_chunks):
    tb = n_chunks * CHUNK
    nh = N_REC_HEADS

    @pl.when(pl.program_id(0) == 0)
    def _():
        s_ref[...] = jnp.zeros_like(s_ref)
        tail_ref[...] = jnp.zeros_like(tail_ref)

    def conv_silu(u_ref, slot, h):
        cols = slice(h * HEAD_W, (h + 1) * HEAD_W)
        u = u_ref[h]
        ext = jnp.concatenate([tail_ref[slot, :, cols], u], axis=0)
        w = cw_ref[:, slot * nh * HEAD_W + h * HEAD_W:slot * nh * HEAD_W + (h + 1) * HEAD_W]
        acc = u * w[CONV_W - 1:CONV_W, :]
        for j in range(CONV_W - 1):
            acc = acc + pltpu.roll(ext, CONV_W - 1 - j, axis=0)[8:, :] * w[j:j + 1, :]
        tail_ref[slot, :, cols] = u[tb - 8:, :]
        return _silu(acc)

    for h in range(nh):
        qh = conv_silu(gq_ref, 0, h)
        kh = conv_silu(gk_ref, 1, h)
        q_s[h] = qh * lax.rsqrt(jnp.sum(qh * qh, axis=-1, keepdims=True) + EPS) * (HEAD_W ** -0.5)
        k_s[h] = kh * lax.rsqrt(jnp.sum(kh * kh, axis=-1, keepdims=True) + EPS)
        v_s[h] = conv_silu(gv_ref, 2, h)
    gab = gab_ref[...].T
    g = -jnp.exp(alog_ref[...]) * _softplus(gab[:nh, :] + dtb_ref[...])
    brow = lax.broadcasted_iota(jnp.int32, (tb, tb), 0)
    bcol = lax.broadcasted_iota(jnp.int32, (tb, tb), 1)
    same_chunk_upto = jnp.logical_and(brow <= bcol, brow // CHUNK == bcol // CHUNK).astype(F32)
    g_s[...] = _dot(g, same_chunk_upto, precision=HI)
    beta_s[...] = _sigmoid(gab[nh:2 * nh, :])

    row = lax.broadcasted_iota(jnp.int32, (CHUNK, CHUNK), 0)
    col = lax.broadcasted_iota(jnp.int32, (CHUNK, CHUNK), 1)
    tril = col <= row
    strict = col < row
    eye = col == row
    eye_f = eye.astype(F32)

    def wy_heads(hg, carry):
        items = [(hg * WY_GROUP + dh, c) for dh in range(WY_GROUP) for c in range(n_chunks)]
        n = range(len(items))
        rows = [slice(c * CHUNK, (c + 1) * CHUNK) for _, c in items]
        q = [q_s[h, rows[x], :] for x, (h, _) in enumerate(items)]
        k = [k_s[h, rows[x], :] for x, (h, _) in enumerate(items)]
        gc_all = [g_s[pl.ds(hg * WY_GROUP + dh, 1), :] for dh in range(WY_GROUP)]
        beta_all = [beta_s[pl.ds(hg * WY_GROUP + dh, 1), :] for dh in range(WY_GROUP)]
        gc_row = [jnp.broadcast_to(gc_all[x // n_chunks][:, rows[x]], (CHUNK, CHUNK)) for x in n]
        beta_b = [jnp.broadcast_to(beta_all[x // n_chunks][:, rows[x]], (CHUNK, CHUNK)) for x in n]
        gc_col = [jnp.sum(jnp.where(eye, gc_row[x], 0.0), axis=-1, keepdims=True) for x in n]
        beta_col = [jnp.sum(jnp.where(eye, beta_b[x], 0.0), axis=-1, keepdims=True) for x in n]
        decay = [jnp.where(tril, jnp.exp(jnp.minimum(gc_col[x] - gc_row[x], 0.0)), 0.0) for x in n]
        kb = [k[x] * beta_col[x] for x in n]
        prod = [_dot_nt(jnp.concatenate([kb[x], q[x]], axis=0).astype(BF16), k[x].astype(BF16))
                for x in n]
        n1 = [jnp.where(strict, -prod[x][:CHUNK] * decay[x], 0.0) for x in n]
        inv = [eye_f + n1[x] for x in n]
        npow = [n1[x].astype(BF16) for x in n]
        for _ in range(5):
            npow = [_dot(npow[x], npow[x]).astype(BF16) for x in n]
            inv = [inv[x] + _dot(inv[x].astype(BF16), npow[x]) for x in n]
        e_gc = [jnp.exp(gc_col[x]) for x in n]
        rhs = [jnp.concatenate([v_s[h, rows[x], :] * beta_col[x], kb[x] * e_gc[x]], axis=1)
               for x, (h, _) in enumerate(items)]
        sol = [_dot(inv[x].astype(BF16), rhs[x].astype(BF16)) for x in n]
        for x, (h, c) in enumerate(items):
            g_last = gc_col[x][CHUNK - 1:CHUNK, :]
            u_s[h, rows[x], :] = sol[x][:, :HEAD_W]
            wq_s[h, c, :CHUNK, :] = sol[x][:, HEAD_W:].astype(BF16)
            wq_s[h, c, CHUNK:, :] = (q[x] * e_gc[x]).astype(BF16)
            kdt_s[h, c] = (k[x] * jnp.exp(g_last - gc_col[x])).T.astype(BF16)
            qk_s[h, c] = (prod[x][CHUNK:] * decay[x]).astype(BF16)
            gl_s[h, c] = jnp.broadcast_to(jnp.exp(g_last), (8, HEAD_W))
        return carry

    lax.fori_loop(0, nh // WY_GROUP, wy_heads, 0)

    heads = range(nh)
    s = [s_ref[h] for h in heads]
    for c in range(n_chunks):
        lo, hi = c * CHUNK, (c + 1) * CHUNK
        ws = [_dot(wq_s[h, c], s[h].astype(BF16)) for h in heads]
        v_new = [(u_s[h, lo:hi, :] - ws[h][:CHUNK]).astype(BF16) for h in heads]
        o = [ws[h][CHUNK:] + _dot(qk_s[h, c], v_new[h]) for h in heads]
        s = [s[h] * gl_s[h, c][:1, :] + _dot(kdt_s[h, c], v_new[h]) for h in heads]
        for h in heads:
            y = o[h] * lax.rsqrt(jnp.mean(o[h] * o[h], axis=-1, keepdims=True) + EPS)
            o_ref[lo:hi, h * HEAD_W:(h + 1) * HEAD_W] = (
                y * nw_ref[...] * _silu(gg_ref[h, lo:hi, :])).astype(o_ref.dtype)
    for h in heads:
        s_ref[h] = s[h]


def gdn_heads(proj, gates, conv_w, a_log, dt_bias, norm_w, tb):
    t = proj.shape[1]
    nh = N_REC_HEADS
    gw = nh * HEAD_W
    n_chunks = tb // CHUNK

    def group(block):
        return pl.BlockSpec((nh, tb, HEAD_W), lambda i: (block, i, 0))

    def whole(a):
        return pl.BlockSpec(a.shape, lambda i: (0,) * a.ndim)

    a_log = a_log.reshape(nh, 1)
    dt_bias = dt_bias.reshape(nh, 1)
    norm_w = norm_w.reshape(1, HEAD_W)
    return pl.pallas_call(
        functools.partial(_gdn_kernel, n_chunks=n_chunks),
        grid=(t // tb,),
        in_specs=[whole(a_log), whole(dt_bias), group(4), group(5), group(6), group(7),
                  pl.BlockSpec((tb, HEAD_W), lambda i: (i, 0)), whole(conv_w), whole(norm_w)],
        out_specs=pl.BlockSpec((tb, gw), lambda i: (i, 0)),
        out_shape=jax.ShapeDtypeStruct((t, gw), BF16),
        scratch_shapes=[pltpu.VMEM((nh, HEAD_W, HEAD_W), F32),
                        pltpu.VMEM((3, 8, gw), F32),
                        pltpu.VMEM((nh, tb, HEAD_W), F32),
                        pltpu.VMEM((nh, tb, HEAD_W), F32),
                        pltpu.VMEM((nh, tb, HEAD_W), F32),
                        pltpu.VMEM((nh, tb), F32),
                        pltpu.VMEM((nh, tb), F32),
                        pltpu.VMEM((nh, tb, HEAD_W), F32),
                        pltpu.VMEM((nh, n_chunks, 2 * CHUNK, HEAD_W), BF16),
                        pltpu.VMEM((nh, n_chunks, HEAD_W, CHUNK), BF16),
                        pltpu.VMEM((nh, n_chunks, CHUNK, CHUNK), BF16),
                        pltpu.VMEM((nh, n_chunks, 8, HEAD_W), F32)],
        compiler_params=_cparams(("arbitrary",)),
        name="gdn_heads",
    )(a_log, dt_bias, proj, proj, proj, proj, gates, conv_w, norm_w)


def _sb_kernel(q_ref, k_ref, v_ref, o_ref, acc_ref, c_ref, *, n_sub):
    blk = SB_BLK
    qb0 = pl.program_id(1) * n_sub
    row = lax.broadcasted_iota(jnp.int32, (2 * blk, 2 * blk), 0)
    col = lax.broadcasted_iota(jnp.int32, (2 * blk, 2 * blk), 1)
    later2 = (row > col).astype(BF16)
    later = later2[:blk, :blk]
    causal = (col < row)[:blk, :blk]

    def tiles(subs, s, masked, watch=None, width=1):
        n = range(len(subs))
        kw = width * blk
        start = [pl.multiple_of((qb0 + i - s - (width - 1)) * blk, blk) for i in subs]
        z = [_dot_nt(q_ref[i * blk:(i + 1) * blk, :], k_ref[pl.ds(start[x], kw), :])
             for x, i in enumerate(subs)]
        sp = [_softplus(z[x]) for x in n]
        log_1m = [jnp.where(causal, -sp[x], 0.0) if masked else -sp[x] for x in n]
        after = [_dot(log_1m[x].astype(BF16), later if width == 1 else later2)
                 for x in n]
        c = [c_ref[i] for i in subs]
        a = [jnp.exp((z[x] - sp[x]) + after[x] + c[x]) for x in n]
        if masked:
            a = [jnp.where(causal, a[x], 0.0) for x in n]
        c_new = [c[x] + after[x][:, :1] + log_1m[x][:, :1] for x in n]
        for x, i in enumerate(subs):
            acc_ref[i] += _dot(a[x].astype(BF16), v_ref[pl.ds(start[x], kw), :])
            c_ref[i] = c_new[x]
        watched = [c_new[x] for x, i in enumerate(subs) if watch is None or i in watch]
        if not watched:
            return jnp.float32(-jnp.inf)
        return jnp.max(functools.reduce(jnp.maximum, watched))

    all_subs = list(range(n_sub))
    acc_ref[...] = jnp.zeros_like(acc_ref)
    c_ref[...] = jnp.zeros_like(c_ref)
    c_max = tiles(all_subs, 0, True)

    def finish(i, s_start):
        lax.while_loop(
            lambda carry: jnp.logical_and(carry[0] <= qb0 + i, carry[1] > EXP_ZERO_BELOW),
            lambda carry: (carry[0] + 1, tiles([i], carry[0], False)),
            (s_start, jnp.max(c_ref[i])))

    @pl.when(qb0 > 0)
    def _():
        s_end, c_left = lax.while_loop(
            lambda carry: jnp.logical_and(carry[0] + (SB_SWEEP - 1) <= qb0, carry[1] > EXP_ZERO_BELOW),
            lambda carry: (carry[0] + SB_SWEEP, tiles(all_subs, carry[0], False, width=SB_SWEEP)),
            (jnp.int32(1), c_max))

        @pl.when(c_left > EXP_ZERO_BELOW)
        def _():
            for i in range(1, n_sub):
                finish(i, s_end)

    def first_block_sweep(s, c_left):
        if s < n_sub:
            @pl.when(c_left > EXP_ZERO_BELOW)
            def _():
                subs = list(range(s, n_sub))
                if s <= SB_FIRST_UNROLL:
                    first_block_sweep(s + 1, tiles(subs, s, False, watch=subs[1:]))
                else:
                    for i in subs:
                        finish(i, jnp.int32(s))

    @pl.when(qb0 == 0)
    def _():
        first_block_sweep(1, c_max)

    for i in range(n_sub):
        o_ref[i * blk:(i + 1) * blk, :] = acc_ref[i].astype(o_ref.dtype)


def sb_attention(qkv, n_heads, n_sub):
    t = qkv.shape[1]
    qb = n_sub * SB_BLK
    return pl.pallas_call(
        functools.partial(_sb_kernel, n_sub=n_sub),
        grid=(n_heads, t // qb),
        in_specs=[pl.BlockSpec((None, qb, HEAD_W), lambda h, i: (h, i, 0)),
                  pl.BlockSpec((None, t, HEAD_W), lambda h, i: (n_heads + h, 0, 0)),
                  pl.BlockSpec((None, t, HEAD_W), lambda h, i: (2 * n_heads + h, 0, 0))],
        out_specs=pl.BlockSpec((qb, HEAD_W), lambda h, i: (i, h)),
        out_shape=jax.ShapeDtypeStruct((t, n_heads * HEAD_W), BF16),
        scratch_shapes=[pltpu.VMEM((n_sub, SB_BLK, HEAD_W), F32),
                        pltpu.VMEM((n_sub, SB_BLK, 1), F32)],
        compiler_params=_cparams(("parallel", "arbitrary")),
        name="sb_attention",
    )(qkv, qkv, qkv)


def _pick(n, candidates):
    for c in candidates:
        if n % c == 0:
            return c
    return n


def kernel(x, mix_norm, a_w_in, a_conv_w, a_a_log, a_dt_bias, a_lb_logits, a_hgrn_norm,
           a_gdn_norm, a_w_out, c_w_qkv, c_w_o, mlp_norm, mlp_w1, mlp_w2, final_norm):
    bsz, t, d = x.shape
    assert bsz == 1
    depth = mix_norm.shape[0]
    nh = N_REC_HEADS
    kw = nh * HEAD_W
    xs = x.reshape(t, d)
    tm = _pick(t, (1024, 512, 256, 128, 64))
    tm_proj = _pick(t, (512, 256, 128, 64))
    tn_in = _pick(kw, (1024, 512, 256, 128))
    tb_hgrn = _pick(t, (2048, 1024, 512, 256, 128, 64))
    tb_gdn = _pick(t, (256, 128, 64))
    sb_heads = d // HEAD_W
    sb_sub = _pick(t // SB_BLK, (16, 8, 4, 2, 1))

    lb_all = jnp.cumsum(jax.nn.softmax(a_lb_logits.astype(F32), axis=0), axis=0)
    mlp_w1_b, mlp_w2_b = mlp_w1.astype(BF16), mlp_w2.astype(BF16)
    for layer in range(depth):
        j = layer // 2
        if layer % 2 == 0:
            w_in = a_w_in[j]
            n_gate = 2 * nh
            w_in_b = w_in.astype(BF16)
            w_gate = jnp.pad(w_in_b[:, 7 * kw:7 * kw + n_gate], ((0, 0), (0, HEAD_W - n_gate)))
            w_parts = [(w_in_b, 7 * kw // tn_in), (w_in_b[:, 7 * kw + n_gate:], kw // tn_in)]
            proj, gates = norm_matmul(xs, mix_norm[layer], w_parts, jnp.ones((8 * kw,), F32), F32,
                                      tm, tn_in, w_extra=w_gate)
            o_a = hgrn_heads(proj, lb_all[j], a_hgrn_norm[j], tb_hgrn)
            o_b = gdn_heads(proj, gates, a_conv_w[j], a_a_log[j], a_dt_bias[j], a_gdn_norm[j], tb_gdn)
            w_out = a_w_out[j].astype(BF16)
            xs = proj_residual([o_a, o_b], [w_out[:kw], w_out[kw:]], xs, tm_proj, d)
        else:
            scale = jnp.concatenate([jnp.full((d,), HEAD_W ** -0.5, F32), jnp.ones((2 * d,), F32)])
            qkv, = norm_matmul(xs, mix_norm[layer], [(c_w_qkv[j].astype(BF16), 3 * d // tn_in)], scale, BF16,
                               tm, tn_in)
            o_c = sb_attention(qkv, sb_heads, sb_sub)
            xs = proj_residual([o_c], [c_w_o[j].astype(BF16)], xs, tm_proj, d)
        last = layer == depth - 1
        xs = mlp_residual(xs, mlp_norm[layer], mlp_w1_b, mlp_w2_b, layer,
                          final_norm, last, tm, _pick(mlp_w1.shape[2], (512,)))
    return xs.reshape(bsz, t, d)
```

```python
import functools

import jax
import jax.numpy as jnp
from jax import lax
from jax.experimental import pallas as pl
from jax.experimental.pallas import tpu as pltpu

F32 = jnp.float32
BF16 = jnp.bfloat16
EPS = 1e-6
HEAD_W = 128
CHUNK = 64
SUB = 16
CONV_W = 4
SB_BLK = 128
SB_SWEEP = 2
SB_FIRST_UNROLL = 2
N_REC_HEADS = 8
WY_GROUP = 8
VMEM_LIMIT = 56 * 1024 * 1024
HI = lax.Precision.HIGHEST
EXP_ZERO_BELOW = -104.0


def _cparams(sem):
    return pltpu.CompilerParams(dimension_semantics=sem, vmem_limit_bytes=VMEM_LIMIT)


def _dot(a, b, precision=None):
    return jnp.dot(a, b, preferred_element_type=F32, precision=precision)


def _dot_nt(a, b, precision=None):
    return lax.dot_general(a, b, (((1,), (1,)), ((), ())),
                           preferred_element_type=F32, precision=precision)


def _rms(x, gain):
    return x * lax.rsqrt(jnp.mean(x * x, axis=-1, keepdims=True) + EPS) * gain


def _sigmoid(x):
    return 0.5 * jnp.tanh(0.5 * x) + 0.5


def _silu(x):
    return x * _sigmoid(x)


def _softplus(x):
    return jnp.maximum(x, 0.0) + jnp.log(1.0 + jnp.exp(-jnp.abs(x)))


def _norm_matmul_kernel(*refs, parts, has_extra):
    n_parts = len(parts)
    x_ref, g_ref = refs[:2]
    w_refs = refs[2:2 + n_parts]
    rest = refs[2 + n_parts:]
    if has_extra:
        cs_ref, we_ref, o_ref, oe_ref, h_ref = rest
    else:
        cs_ref, o_ref, h_ref = rest
    j = pl.program_id(1)

    @pl.when(j == 0)
    def _():
        h_ref[...] = _rms(x_ref[...], g_ref[...]).astype(BF16)
        if has_extra:
            oe_ref[...] = _dot(h_ref[...], we_ref[...])

    def emit(w_ref):
        acc = _dot(h_ref[...], w_ref[...]) * cs_ref[...]
        for c in range(o_ref.shape[0]):
            o_ref[c] = acc[:, c * HEAD_W:(c + 1) * HEAD_W].astype(o_ref.dtype)

    if n_parts == 1:
        emit(w_refs[0])
    else:
        for w_ref, (lo, hi) in zip(w_refs, parts):
            pl.when(jnp.logical_and(j >= lo, j < hi))(functools.partial(emit, w_ref))


def norm_matmul(x, gain, w_parts, col_scale, out_dtype, tm, tn, w_extra=None):
    m, k = x.shape
    has_extra = w_extra is not None
    parts, lo = [], 0
    for _, n_tiles in w_parts:
        parts.append((lo, lo + n_tiles))
        lo += n_tiles
    n = lo * tn
    in_specs = [pl.BlockSpec((tm, k), lambda i, j: (i, 0)),
                pl.BlockSpec((1, k), lambda i, j: (0, 0))]
    for lo, hi in parts:
        in_specs.append(pl.BlockSpec((k, tn), lambda i, j, lo=lo, hi=hi: (0, jnp.clip(j - lo, 0, hi - lo - 1))))
    in_specs.append(pl.BlockSpec((1, tn), lambda i, j: (0, j)))
    out_specs = [pl.BlockSpec((tn // HEAD_W, tm, HEAD_W), lambda i, j: (j, i, 0))]
    out_shape = [jax.ShapeDtypeStruct((n // HEAD_W, m, HEAD_W), out_dtype)]
    args = [x, gain.reshape(1, k)] + [w for w, _ in w_parts] + [col_scale.reshape(1, n)]
    if has_extra:
        in_specs.append(pl.BlockSpec((k, HEAD_W), lambda i, j: (0, 0)))
        out_specs.append(pl.BlockSpec((tm, HEAD_W), lambda i, j: (i, 0)))
        out_shape.append(jax.ShapeDtypeStruct((m, HEAD_W), F32))
        args.append(w_extra)
    return pl.pallas_call(
        functools.partial(_norm_matmul_kernel, parts=tuple(parts), has_extra=has_extra),
        grid=(m // tm, n // tn),
        in_specs=in_specs,
        out_specs=out_specs,
        out_shape=out_shape,
        scratch_shapes=[pltpu.VMEM((tm, k), BF16)],
        compiler_params=_cparams(("parallel", "arbitrary")),
        name="norm_matmul",
    )(*args)


def _proj_residual_kernel(*refs, n_lhs):
    lhs = refs[:n_lhs]
    ws = refs[n_lhs:2 * n_lhs]
    x_ref, o_ref = refs[2 * n_lhs], refs[2 * n_lhs + 1]
    acc = x_ref[...]
    for a_ref, w_ref in zip(lhs, ws):
        acc = acc + _dot(a_ref[...], w_ref[...])
    o_ref[...] = acc


def proj_residual(lhs_list, w_list, x, tm, tn):
    m, n = x.shape
    n_lhs = len(lhs_list)
    in_specs = [pl.BlockSpec((tm, a.shape[1]), lambda i, j: (i, 0)) for a in lhs_list]
    in_specs += [pl.BlockSpec((w.shape[0], tn), lambda i, j: (0, j)) for w in w_list]
    in_specs += [pl.BlockSpec((tm, tn), lambda i, j: (i, j))]
    return pl.pallas_call(
        functools.partial(_proj_residual_kernel, n_lhs=n_lhs),
        grid=(m // tm, n // tn),
        in_specs=in_specs,
        out_specs=pl.BlockSpec((tm, tn), lambda i, j: (i, j)),
        out_shape=jax.ShapeDtypeStruct((m, n), F32),
        compiler_params=_cparams(("parallel", "arbitrary")),
        name="proj_residual",
    )(*lhs_list, *w_list, x)


def _mlp_kernel(x_ref, g_ref, w1_ref, w2_ref, fg_ref, o_ref, h_ref, *, final_norm):
    j = pl.program_id(1)

    @pl.when(j == 0)
    def _():
        x = x_ref[...]
        h_ref[...] = _rms(x, g_ref[...]).astype(BF16)
        o_ref[...] = x

    a = _dot(h_ref[...], w1_ref[...])
    a = jnp.square(jnp.maximum(a, 0.0)).astype(BF16)
    o_ref[...] += _dot(a, w2_ref[...])

    if final_norm:
        @pl.when(j == pl.num_programs(1) - 1)
        def _():
            o_ref[...] = _rms(o_ref[...], fg_ref[...])


def mlp_residual(x, gain, w1, w2, layer, final_gain, final_norm, tm, tf):
    m, d = x.shape
    ff = w1.shape[2]
    return pl.pallas_call(
        functools.partial(_mlp_kernel, final_norm=final_norm),
        grid=(m // tm, ff // tf),
        in_specs=[
            pl.BlockSpec((tm, d), lambda i, j: (i, 0), pipeline_mode=pl.Buffered(1)),
            pl.BlockSpec((1, d), lambda i, j: (0, 0)),
            pl.BlockSpec((None, d, tf), lambda i, j: (layer, 0, j)),
            pl.BlockSpec((None, tf, d), lambda i, j: (layer, j, 0)),
            pl.BlockSpec((1, d), lambda i, j: (0, 0)),
        ],
        out_specs=pl.BlockSpec((tm, d), lambda i, j: (i, 0)),
        out_shape=jax.ShapeDtypeStruct((m, d), F32),
        scratch_shapes=[pltpu.VMEM((tm, d), BF16)],
        compiler_params=_cparams(("parallel", "arbitrary")),
        name="mlp_residual",
    )(x, gain.reshape(1, d), w1, w2, final_gain.reshape(1, d))


def _hgrn_kernel(hq_ref, hf_ref, hi_ref, hg_ref, lb_ref, nw_ref, o_ref, st_ref, *, n_chunks):
    @pl.when(pl.program_id(1) == 0)
    def _():
        st_ref[...] = jnp.zeros_like(st_ref)

    chunks = range(n_chunks)
    n_sub = CHUNK // SUB
    lb = lb_ref[...]
    row = lax.broadcasted_iota(jnp.int32, (CHUNK, CHUNK), 0)
    col = lax.broadcasted_iota(jnp.int32, (CHUNK, CHUNK), 1)
    tril = (col <= row).astype(F32)

    hf = hf_ref[...]
    log_f = jnp.log(lb + (1.0 - lb) * _sigmoid(hf))
    k_all = (1.0 - lb) * _sigmoid(-hf)
    q_all = _silu(hq_ref[...])
    vb_all = hi_ref[...].astype(BF16)

    rows = [slice(c * CHUNK, (c + 1) * CHUNK) for c in chunks]
    q = [q_all[r] for r in rows]
    k = [k_all[r] for r in rows]
    vb = [vb_all[r] for r in rows]
    b = [_dot(tril, log_f[r], precision=HI) for r in rows]
    o_intra = [[] for _ in chunks]
    for i in range(n_sub):
        lo, hi = i * SUB, (i + 1) * SUB
        srow = lax.broadcasted_iota(jnp.int32, (SUB, hi), 0)
        scol = lax.broadcasted_iota(jnp.int32, (SUB, hi), 1)
        causal = scol <= srow + lo
        b_ref = [b[c][lo:lo + 1, :] for c in chunks]
        qs = [(q[c][lo:hi] * jnp.exp(b[c][lo:hi] - b_ref[c])).astype(BF16) for c in chunks]
        ks = [(k[c][:hi] * jnp.exp(b_ref[c] - b[c][:hi])).astype(BF16) for c in chunks]
        att = [jnp.where(causal, _dot_nt(qs[c], ks[c]), 0.0).astype(BF16) for c in chunks]
        for c in chunks:
            o_intra[c].append(_dot(att[c], vb[c][:hi]))
    b_last = [b[c][CHUNK - 1:CHUNK, :] for c in chunks]
    qd = [(q[c] * jnp.exp(b[c])).astype(BF16) for c in chunks]
    kd = [(k[c] * jnp.exp(b_last[c] - b[c])).astype(BF16) for c in chunks]
    p = [_dot(vb[c].T, kd[c]) for c in chunks]
    d = [jnp.exp(b_last[c]) for c in chunks]
    gate = nw_ref[...] * _silu(hg_ref[...])

    st = st_ref[...]
    for c in chunks:
        o = jnp.concatenate(o_intra[c], axis=0) + _dot_nt(qd[c], st.astype(BF16))
        st = st * d[c] + p[c]
        y = o * lax.rsqrt(jnp.mean(o * o, axis=-1, keepdims=True) + EPS)
        o_ref[rows[c], :] = (y * gate[rows[c]]).astype(o_ref.dtype)
    st_ref[...] = st


def hgrn_heads(proj, lb, norm_w, tb):
    t = proj.shape[1]
    nh = N_REC_HEADS

    def col(block0):
        return pl.BlockSpec((None, tb, HEAD_W), lambda h, i: (block0 + h, i, 0))

    return pl.pallas_call(
        functools.partial(_hgrn_kernel, n_chunks=tb // CHUNK),
        grid=(nh, t // tb),
        in_specs=[col(0), col(nh), col(2 * nh), col(3 * nh),
                  pl.BlockSpec((None, 1, HEAD_W), lambda h, i: (h, 0, 0)),
                  pl.BlockSpec((1, HEAD_W), lambda h, i: (0, 0))],
        out_specs=pl.BlockSpec((tb, HEAD_W), lambda h, i: (i, h)),
        out_shape=jax.ShapeDtypeStruct((t, nh * HEAD_W), BF16),
        scratch_shapes=[pltpu.VMEM((HEAD_W, HEAD_W), F32)],
        compiler_params=_cparams(("parallel", "arbitrary")),
        name="hgrn_heads",
    )(proj, proj, proj, proj, lb.reshape(nh, 1, HEAD_W), norm_w.reshape(1, HEAD_W))


def _gdn_kernel(alog_ref, dtb_ref, gq_ref, gk_ref, gv_ref, gg_ref, gab_ref, cw_ref, nw_ref, o_ref,
                s_ref, tail_ref, q_s, k_s, v_s, g_s, beta_s, u_s, wq_s, kdt_s, qk_s, gl_s, *, n_chunks):
    tb = n_chunks * CHUNK
    nh = N_REC_HEADS

    @pl.when(pl.program_id(0) == 0)
    def _():
        s_ref[...] = jnp.zeros_like(s_ref)
        tail_ref[...] = jnp.zeros_like(tail_ref)

    def conv_silu(u_ref, slot, h):
        cols = slice(h * HEAD_W, (h + 1) * HEAD_W)
        u = u_ref[h]
        ext = jnp.concatenate([tail_ref[slot, :, cols], u], axis=0)
        w = cw_ref[:, slot * nh * HEAD_W + h * HEAD_W:slot * nh * HEAD_W + (h + 1) * HEAD_W]
        acc = u * w[CONV_W - 1:CONV_W, :]
        for j in range(CONV_W - 1):
            acc = acc + pltpu.roll(ext, CONV_W - 1 - j, axis=0)[8:, :] * w[j:j + 1, :]
        tail_ref[slot, :, cols] = u[tb - 8:, :]
        return _silu(acc)

    for h in range(nh):
        qh = conv_silu(gq_ref, 0, h)
        kh = conv_silu(gk_ref, 1, h)
        q_s[h] = qh * lax.rsqrt(jnp.sum(qh * qh, axis=-1, keepdims=True) + EPS) * (HEAD_W ** -0.5)
        k_s[h] = kh * lax.rsqrt(jnp.sum(kh * kh, axis=-1, keepdims=True) + EPS)
        v_s[h] = conv_silu(gv_ref, 2, h)
    gab = gab_ref[...].T
    g = -jnp.exp(alog_ref[...]) * _softplus(gab[:nh, :] + dtb_ref[...])
    brow = lax.broadcasted_iota(jnp.int32, (tb, tb), 0)
    bcol = lax.broadcasted_iota(jnp.int32, (tb, tb), 1)
    same_chunk_upto = jnp.logical_and(brow <= bcol, brow // CHUNK == bcol // CHUNK).astype(F32)
    g_s[...] = _dot(g, same_chunk_upto, precision=HI)
    beta_s[...] = _sigmoid(gab[nh:2 * nh, :])

    row = lax.broadcasted_iota(jnp.int32, (CHUNK, CHUNK), 0)
    col = lax.broadcasted_iota(jnp.int32, (CHUNK, CHUNK), 1)
    tril = col <= row
    strict = col < row
    eye = col == row
    eye_f = eye.astype(F32)

    def wy_heads(hg, carry):
        items = [(hg * WY_GROUP + dh, c) for dh in range(WY_GROUP) for c in range(n_chunks)]
        n = range(len(items))
        rows = [slice(c * CHUNK, (c + 1) * CHUNK) for _, c in items]
        q = [q_s[h, rows[x], :] for x, (h, _) in enumerate(items)]
        k = [k_s[h, rows[x], :] for x, (h, _) in enumerate(items)]
        gc_all = [g_s[pl.ds(hg * WY_GROUP + dh, 1), :] for dh in range(WY_GROUP)]
        beta_all = [beta_s[pl.ds(hg * WY_GROUP + dh, 1), :] for dh in range(WY_GROUP)]
        gc_row = [jnp.broadcast_to(gc_all[x // n_chunks][:, rows[x]], (CHUNK, CHUNK)) for x in n]
        beta_b = [jnp.broadcast_to(beta_all[x // n_chunks][:, rows[x]], (CHUNK, CHUNK)) for x in n]
        gc_col = [jnp.sum(jnp.where(eye, gc_row[x], 0.0), axis=-1, keepdims=True) for x in n]
        beta_col = [jnp.sum(jnp.where(eye, beta_b[x], 0.0), axis=-1, keepdims=True) for x in n]
        decay = [jnp.where(tril, jnp.exp(jnp.minimum(gc_col[x] - gc_row[x], 0.0)), 0.0) for x in n]
        kb = [k[x] * beta_col[x] for x in n]
        prod = [_dot_nt(jnp.concatenate([kb[x], q[x]], axis=0).astype(BF16), k[x].astype(BF16))
                for x in n]
        n1 = [jnp.where(strict, -prod[x][:CHUNK] * decay[x], 0.0) for x in n]
        inv = [eye_f + n1[x] for x in n]
        npow = [n1[x].astype(BF16) for x in n]
        for _ in range(5):
            npow = [_dot(npow[x], npow[x]).astype(BF16) for x in n]
            inv = [inv[x] + _dot(inv[x].astype(BF16), npow[x]) for x in n]
        e_gc = [jnp.exp(gc_col[x]) for x in n]
        rhs = [jnp.concatenate([v_s[h, rows[x], :] * beta_col[x], kb[x] * e_gc[x]], axis=1)
               for x, (h, _) in enumerate(items)]
        sol = [_dot(inv[x].astype(BF16), rhs[x].astype(BF16)) for x in n]
        for x, (h, c) in enumerate(items):
            g_last = gc_col[x][CHUNK - 1:CHUNK, :]
            u_s[h, rows[x], :] = sol[x][:, :HEAD_W]
            wq_s[h, c, :CHUNK, :] = sol[x][:, HEAD_W:].astype(BF16)
            wq_s[h, c, CHUNK:, :] = (q[x] * e_gc[x]).astype(BF16)
            kdt_s[h, c] = (k[x] * jnp.exp(g_last - gc_col[x])).T.astype(BF16)
            qk_s[h, c] = (prod[x][CHUNK:] * decay[x]).astype(BF16)
            gl_s[h, c] = jnp.broadcast_to(jnp.exp(g_last), (8, HEAD_W))
        return carry

    lax.fori_loop(0, nh // WY_GROUP, wy_heads, 0)

    heads = range(nh)
    s = [s_ref[h] for h in heads]
    for c in range(n_chunks):
        lo, hi = c * CHUNK, (c + 1) * CHUNK
        ws = [_dot(wq_s[h, c], s[h].astype(BF16)) for h in heads]
        v_new = [(u_s[h, lo:hi, :] - ws[h][:CHUNK]).astype(BF16) for h in heads]
        o = [ws[h][CHUNK:] + _dot(qk_s[h, c], v_new[h]) for h in heads]
        s = [s[h] * gl_s[h, c][:1, :] + _dot(kdt_s[h, c], v_new[h]) for h in heads]
        for h in heads:
            y = o[h] * lax.rsqrt(jnp.mean(o[h] * o[h], axis=-1, keepdims=True) + EPS)
            o_ref[lo:hi, h * HEAD_W:(h + 1) * HEAD_W] = (
                y * nw_ref[...] * _silu(gg_ref[h, lo:hi, :])).astype(o_ref.dtype)
    for h in heads:
        s_ref[h] = s[h]


def gdn_heads(proj, gates, conv_w, a_log, dt_bias, norm_w, tb):
    t = proj.shape[1]
    nh = N_REC_HEADS
    gw = nh * HEAD_W
    n_chunks = tb // CHUNK

    def group(block):
        return pl.BlockSpec((nh, tb, HEAD_W), lambda i: (block, i, 0))

    def whole(a):
        return pl.BlockSpec(a.shape, lambda i: (0,) * a.ndim)

    a_log = a_log.reshape(nh, 1)
    dt_bias = dt_bias.reshape(nh, 1)
    norm_w = norm_w.reshape(1, HEAD_W)
    return pl.pallas_call(
        functools.partial(_gdn_kernel, n_chunks=n_chunks),
        grid=(t // tb,),
        in_specs=[whole(a_log), whole(dt_bias), group(4), group(5), group(6), group(7),
                  pl.BlockSpec((tb, HEAD_W), lambda i: (i, 0)), whole(conv_w), whole(norm_w)],
        out_specs=pl.BlockSpec((tb, gw), lambda i: (i, 0)),
        out_shape=jax.ShapeDtypeStruct((t, gw), BF16),
        scratch_shapes=[pltpu.VMEM((nh, HEAD_W, HEAD_W), F32),
                        pltpu.VMEM((3, 8, gw), F32),
                        pltpu.VMEM((nh, tb, HEAD_W), F32),
                        pltpu.VMEM((nh, tb, HEAD_W), F32),
                        pltpu.VMEM((nh, tb, HEAD_W), F32),
                        pltpu.VMEM((nh, tb), F32),
                        pltpu.VMEM((nh, tb), F32),
                        pltpu.VMEM((nh, tb, HEAD_W), F32),
                        pltpu.VMEM((nh, n_chunks, 2 * CHUNK, HEAD_W), BF16),
                        pltpu.VMEM((nh, n_chunks, HEAD_W, CHUNK), BF16),
                        pltpu.VMEM((nh, n_chunks, CHUNK, CHUNK), BF16),
                        pltpu.VMEM((nh, n_chunks, 8, HEAD_W), F32)],
        compiler_params=_cparams(("arbitrary",)),
        name="gdn_heads",
    )(a_log, dt_bias, proj, proj, proj, proj, gates, conv_w, norm_w)


def _sb_kernel(q_ref, k_ref, v_ref, o_ref, acc_ref, c_ref, *, n_sub):
    blk = SB_BLK
    qb0 = pl.program_id(1) * n_sub
    row = lax.broadcasted_iota(jnp.int32, (2 * blk, 2 * blk), 0)
    col = lax.broadcasted_iota(jnp.int32, (2 * blk, 2 * blk), 1)
    later2 = (row > col).astype(BF16)
    later = later2[:blk, :blk]
    causal = (col < row)[:blk, :blk]

    def tiles(subs, s, masked, watch=None, width=1):
        n = range(len(subs))
        kw = width * blk
        start = [pl.multiple_of((qb0 + i - s - (width - 1)) * blk, blk) for i in subs]
        z = [_dot_nt(q_ref[i * blk:(i + 1) * blk, :], k_ref[pl.ds(start[x], kw), :])
             for x, i in enumerate(subs)]
        sp = [_softplus(z[x]) for x in n]
        log_1m = [jnp.where(causal, -sp[x], 0.0) if masked else -sp[x] for x in n]
        after = [_dot(log_1m[x].astype(BF16), later if width == 1 else later2)
                 for x in n]
        c = [c_ref[i] for i in subs]
        a = [jnp.exp((z[x] - sp[x]) + after[x] + c[x]) for x in n]
        if masked:
            a = [jnp.where(causal, a[x], 0.0) for x in n]
        c_new = [c[x] + after[x][:, :1] + log_1m[x][:, :1] for x in n]
        for x, i in enumerate(subs):
            acc_ref[i] += _dot(a[x].astype(BF16), v_ref[pl.ds(start[x], kw), :])
            c_ref[i] = c_new[x]
        watched = [c_new[x] for x, i in enumerate(subs) if watch is None or i in watch]
        if not watched:
            return jnp.float32(-jnp.inf)
        return jnp.max(functools.reduce(jnp.maximum, watched))

    all_subs = list(range(n_sub))
    acc_ref[...] = jnp.zeros_like(acc_ref)
    c_ref[...] = jnp.zeros_like(c_ref)
    c_max = tiles(all_subs, 0, True)

    def finish(i, s_start):
        lax.while_loop(
            lambda carry: jnp.logical_and(carry[0] <= qb0 + i, carry[1] > EXP_ZERO_BELOW),
            lambda carry: (carry[0] + 1, tiles([i], carry[0], False)),
            (s_start, jnp.max(c_ref[i])))

    @pl.when(qb0 > 0)
    def _():
        s_end, c_left = lax.while_loop(
            lambda carry: jnp.logical_and(carry[0] + (SB_SWEEP - 1) <= qb0, carry[1] > EXP_ZERO_BELOW),
            lambda carry: (carry[0] + SB_SWEEP, tiles(all_subs, carry[0], False, width=SB_SWEEP)),
            (jnp.int32(1), c_max))

        @pl.when(c_left > EXP_ZERO_BELOW)
        def _():
            for i in range(1, n_sub):
                finish(i, s_end)

    def first_block_sweep(s, c_left):
        if s < n_sub:
            @pl.when(c_left > EXP_ZERO_BELOW)
            def _():
                subs = list(range(s, n_sub))
                if s <= SB_FIRST_UNROLL:
                    first_block_sweep(s + 1, tiles(subs, s, False, watch=subs[1:]))
                else:
                    for i in subs:
                        finish(i, jnp.int32(s))

    @pl.when(qb0 == 0)
    def _():
        first_block_sweep(1, c_max)

    for i in range(n_sub):
        o_ref[i * blk:(i + 1) * blk, :] = acc_ref[i].astype(o_ref.dtype)


def sb_attention(qkv, n_heads, n_sub):
    t = qkv.shape[1]
    qb = n_sub * SB_BLK
    return pl.pallas_call(
        functools.partial(_sb_kernel, n_sub=n_sub),
        grid=(n_heads, t // qb),
        in_specs=[pl.BlockSpec((None, qb, HEAD_W), lambda h, i: (h, i, 0)),
                  pl.BlockSpec((None, t, HEAD_W), lambda h, i: (n_heads + h, 0, 0)),
                  pl.BlockSpec((None, t, HEAD_W), lambda h, i: (2 * n_heads + h, 0, 0))],
        out_specs=pl.BlockSpec((qb, HEAD_W), lambda h, i: (i, h)),
        out_shape=jax.ShapeDtypeStruct((t, n_heads * HEAD_W), BF16),
        scratch_shapes=[pltpu.VMEM((n_sub, SB_BLK, HEAD_W), F32),
                        pltpu.VMEM((n_sub, SB_BLK, 1), F32)],
        compiler_params=_cparams(("parallel", "arbitrary")),
        name="sb_attention",
    )(qkv, qkv, qkv)


def _pick(n, candidates):
    for c in candidates:
        if n % c == 0:
            return c
    return n


def kernel(x, mix_norm, a_w_in, a_conv_w, a_a_log, a_dt_bias, a_lb_logits, a_hgrn_norm,
           a_gdn_norm, a_w_out, c_w_qkv, c_w_o, mlp_norm, mlp_w1, mlp_w2, final_norm):
    bsz, t, d = x.shape
    assert bsz == 1
    depth = mix_norm.shape[0]
    nh = N_REC_HEADS
    kw = nh * HEAD_W
    xs = x.reshape(t, d)
    tm = _pick(t, (1024, 512, 256, 128, 64))
    tm_proj = _pick(t, (512, 256, 128, 64))
    tn_in = _pick(kw, (1024, 512, 256, 128))
    tb_hgrn = _pick(t, (2048, 1024, 512, 256, 128, 64))
    tb_gdn = _pick(t, (256, 128, 64))
    sb_heads = d // HEAD_W
    sb_sub = _pick(t // SB_BLK, (16, 8, 4, 2, 1))

    lb_all = jnp.cumsum(jax.nn.softmax(a_lb_logits.astype(F32), axis=0), axis=0)
    mlp_w1_b, mlp_w2_b = mlp_w1.astype(BF16), mlp_w2.astype(BF16)
    for layer in range(depth):
        j = layer // 2
        if layer % 2 == 0:
            w_in = a_w_in[j]
            n_gate = 2 * nh
            w_in_b = w_in.astype(BF16)
            w_gate = jnp.pad(w_in_b[:, 7 * kw:7 * kw + n_gate], ((0, 0), (0, HEAD_W - n_gate)))
            w_parts = [(w_in_b, 7 * kw // tn_in), (w_in_b[:, 7 * kw + n_gate:], kw // tn_in)]
            proj, gates = norm_matmul(xs, mix_norm[layer], w_parts, jnp.ones((8 * kw,), F32), F32,
                                      tm, tn_in, w_extra=w_gate)
            o_a = hgrn_heads(proj, lb_all[j], a_hgrn_norm[j], tb_hgrn)
            o_b = gdn_heads(proj, gates, a_conv_w[j], a_a_log[j], a_dt_bias[j], a_gdn_norm[j], tb_gdn)
            w_out = a_w_out[j].astype(BF16)
            xs = proj_residual([o_a, o_b], [w_out[:kw], w_out[kw:]], xs, tm_proj, d)
        else:
            scale = jnp.concatenate([jnp.full((d,), HEAD_W ** -0.5, F32), jnp.ones((2 * d,), F32)])
            qkv, = norm_matmul(xs, mix_norm[layer], [(c_w_qkv[j].astype(BF16), 3 * d // tn_in)], scale, BF16,
                               tm, tn_in)
            o_c = sb_attention(qkv, sb_heads, sb_sub)
            xs = proj_residual([o_c], [c_w_o[j].astype(BF16)], xs, tm_proj, d)
        last = layer == depth - 1
        xs = mlp_residual(xs, mlp_norm[layer], mlp_w1_b, mlp_w2_b, layer,
                          final_norm, last, tm, _pick(mlp_w1.shape[2], (1024, 512)))
    return xs.reshape(bsz, t, d)
```

```python
import functools

import jax
import jax.numpy as jnp
from jax import lax
from jax.experimental import pallas as pl
from jax.experimental.pallas import tpu as pltpu

F32 = jnp.float32
BF16 = jnp.bfloat16
EPS = 1e-6
HEAD_W = 128
CHUNK = 64
SUB = 16
CONV_W = 4
SB_BLK = 128
SB_SWEEP = 2
SB_FIRST_UNROLL = 2
N_REC_HEADS = 8
WY_GROUP = 8
VMEM_LIMIT = 56 * 1024 * 1024
HI = lax.Precision.HIGHEST
EXP_ZERO_BELOW = -104.0


def _cparams(sem):
    return pltpu.CompilerParams(dimension_semantics=sem, vmem_limit_bytes=VMEM_LIMIT)


def _dot(a, b, precision=None):
    return jnp.dot(a, b, preferred_element_type=F32, precision=precision)


def _dot_nt(a, b, precision=None):
    return lax.dot_general(a, b, (((1,), (1,)), ((), ())),
                           preferred_element_type=F32, precision=precision)


def _rms(x, gain):
    return x * lax.rsqrt(jnp.mean(x * x, axis=-1, keepdims=True) + EPS) * gain


def _sigmoid(x):
    return 0.5 * jnp.tanh(0.5 * x) + 0.5


def _silu(x):
    return x * _sigmoid(x)


def _softplus(x):
    return jnp.maximum(x, 0.0) + jnp.log(1.0 + jnp.exp(-jnp.abs(x)))


def _norm_matmul_kernel(*refs, parts, has_extra):
    n_parts = len(parts)
    x_ref, g_ref = refs[:2]
    w_refs = refs[2:2 + n_parts]
    rest = refs[2 + n_parts:]
    if has_extra:
        cs_ref, we_ref, o_ref, oe_ref, h_ref = rest
    else:
        cs_ref, o_ref, h_ref = rest
    j = pl.program_id(1)

    @pl.when(j == 0)
    def _():
        h_ref[...] = _rms(x_ref[...], g_ref[...]).astype(BF16)
        if has_extra:
            oe_ref[...] = _dot(h_ref[...], we_ref[...])

    def emit(w_ref):
        acc = _dot(h_ref[...], w_ref[...]) * cs_ref[...]
        for c in range(o_ref.shape[0]):
            o_ref[c] = acc[:, c * HEAD_W:(c + 1) * HEAD_W].astype(o_ref.dtype)

    if n_parts == 1:
        emit(w_refs[0])
    else:
        for w_ref, (lo, hi) in zip(w_refs, parts):
            pl.when(jnp.logical_and(j >= lo, j < hi))(functools.partial(emit, w_ref))


def norm_matmul(x, gain, w_parts, col_scale, out_dtype, tm, tn, w_extra=None):
    m, k = x.shape
    has_extra = w_extra is not None
    parts, lo = [], 0
    for _, n_tiles in w_parts:
        parts.append((lo, lo + n_tiles))
        lo += n_tiles
    n = lo * tn
    in_specs = [pl.BlockSpec((tm, k), lambda i, j: (i, 0)),
                pl.BlockSpec((1, k), lambda i, j: (0, 0))]
    for lo, hi in parts:
        in_specs.append(pl.BlockSpec((k, tn), lambda i, j, lo=lo, hi=hi: (0, jnp.clip(j - lo, 0, hi - lo - 1))))
    in_specs.append(pl.BlockSpec((1, tn), lambda i, j: (0, j)))
    out_specs = [pl.BlockSpec((tn // HEAD_W, tm, HEAD_W), lambda i, j: (j, i, 0))]
    out_shape = [jax.ShapeDtypeStruct((n // HEAD_W, m, HEAD_W), out_dtype)]
    args = [x, gain.reshape(1, k)] + [w for w, _ in w_parts] + [col_scale.reshape(1, n)]
    if has_extra:
        in_specs.append(pl.BlockSpec((k, HEAD_W), lambda i, j: (0, 0)))
        out_specs.append(pl.BlockSpec((tm, HEAD_W), lambda i, j: (i, 0)))
        out_shape.append(jax.ShapeDtypeStruct((m, HEAD_W), F32))
        args.append(w_extra)
    return pl.pallas_call(
        functools.partial(_norm_matmul_kernel, parts=tuple(parts), has_extra=has_extra),
        grid=(m // tm, n // tn),
        in_specs=in_specs,
        out_specs=out_specs,
        out_shape=out_shape,
        scratch_shapes=[pltpu.VMEM((tm, k), BF16)],
        compiler_params=_cparams(("parallel", "arbitrary")),
        name="norm_matmul",
    )(*args)


def _proj_residual_kernel(*refs, n_lhs):
    lhs = refs[:n_lhs]
    ws = refs[n_lhs:2 * n_lhs]
    x_ref, o_ref = refs[2 * n_lhs], refs[2 * n_lhs + 1]
    acc = x_ref[...]
    for a_ref, w_ref in zip(lhs, ws):
        acc = acc + _dot(a_ref[...], w_ref[...])
    o_ref[...] = acc


def proj_residual(lhs_list, w_list, x, tm, tn):
    m, n = x.shape
    n_lhs = len(lhs_list)
    in_specs = [pl.BlockSpec((tm, a.shape[1]), lambda i, j: (i, 0)) for a in lhs_list]
    in_specs += [pl.BlockSpec((w.shape[0], tn), lambda i, j: (0, j)) for w in w_list]
    in_specs += [pl.BlockSpec((tm, tn), lambda i, j: (i, j))]
    return pl.pallas_call(
        functools.partial(_proj_residual_kernel, n_lhs=n_lhs),
        grid=(m // tm, n // tn),
        in_specs=in_specs,
        out_specs=pl.BlockSpec((tm, tn), lambda i, j: (i, j)),
        out_shape=jax.ShapeDtypeStruct((m, n), F32),
        compiler_params=_cparams(("parallel", "arbitrary")),
        name="proj_residual",
    )(*lhs_list, *w_list, x)


def _mlp_kernel(x_ref, g_ref, w1_ref, w2_ref, fg_ref, o_ref, h_ref, *, final_norm):
    j = pl.program_id(1)

    @pl.when(j == 0)
    def _():
        x = x_ref[...]
        h_ref[...] = _rms(x, g_ref[...]).astype(BF16)
        o_ref[...] = x

    a = _dot(h_ref[...], w1_ref[...])
    a = jnp.square(jnp.maximum(a, 0.0)).astype(BF16)
    o_ref[...] += _dot(a, w2_ref[...])

    if final_norm:
        @pl.when(j == pl.num_programs(1) - 1)
        def _():
            o_ref[...] = _rms(o_ref[...], fg_ref[...])


def mlp_residual(x, gain, w1, w2, layer, final_gain, final_norm, tm, tf):
    m, d = x.shape
    ff = w1.shape[2]
    return pl.pallas_call(
        functools.partial(_mlp_kernel, final_norm=final_norm),
        grid=(m // tm, ff // tf),
        in_specs=[
            pl.BlockSpec((tm, d), lambda i, j: (i, 0)),
            pl.BlockSpec((1, d), lambda i, j: (0, 0)),
            pl.BlockSpec((None, d, tf), lambda i, j: (layer, 0, j)),
            pl.BlockSpec((None, tf, d), lambda i, j: (layer, j, 0)),
            pl.BlockSpec((1, d), lambda i, j: (0, 0)),
        ],
        out_specs=pl.BlockSpec((tm, d), lambda i, j: (i, 0)),
        out_shape=jax.ShapeDtypeStruct((m, d), F32),
        scratch_shapes=[pltpu.VMEM((tm, d), BF16)],
        compiler_params=_cparams(("parallel", "arbitrary")),
        name="mlp_residual",
    )(x, gain.reshape(1, d), w1, w2, final_gain.reshape(1, d))


def _hgrn_kernel(hq_ref, hf_ref, hi_ref, hg_ref, lb_ref, nw_ref, o_ref, st_ref, *, n_chunks):
    @pl.when(pl.program_id(1) == 0)
    def _():
        st_ref[...] = jnp.zeros_like(st_ref)

    chunks = range(n_chunks)
    n_sub = CHUNK // SUB
    lb = lb_ref[...]
    row = lax.broadcasted_iota(jnp.int32, (CHUNK, CHUNK), 0)
    col = lax.broadcasted_iota(jnp.int32, (CHUNK, CHUNK), 1)
    tril = (col <= row).astype(F32)

    hf = hf_ref[...]
    log_f = jnp.log(lb + (1.0 - lb) * _sigmoid(hf))
    k_all = (1.0 - lb) * _sigmoid(-hf)
    q_all = _silu(hq_ref[...])
    vb_all = hi_ref[...].astype(BF16)

    rows = [slice(c * CHUNK, (c + 1) * CHUNK) for c in chunks]
    q = [q_all[r] for r in rows]
    k = [k_all[r] for r in rows]
    vb = [vb_all[r] for r in rows]
    b = [_dot(tril, log_f[r], precision=HI) for r in rows]
    o_intra = [[] for _ in chunks]
    for i in range(n_sub):
        lo, hi = i * SUB, (i + 1) * SUB
        srow = lax.broadcasted_iota(jnp.int32, (SUB, hi), 0)
        scol = lax.broadcasted_iota(jnp.int32, (SUB, hi), 1)
        causal = scol <= srow + lo
        b_ref = [b[c][lo:lo + 1, :] for c in chunks]
        qs = [(q[c][lo:hi] * jnp.exp(b[c][lo:hi] - b_ref[c])).astype(BF16) for c in chunks]
        ks = [(k[c][:hi] * jnp.exp(b_ref[c] - b[c][:hi])).astype(BF16) for c in chunks]
        att = [jnp.where(causal, _dot_nt(qs[c], ks[c]), 0.0).astype(BF16) for c in chunks]
        for c in chunks:
            o_intra[c].append(_dot(att[c], vb[c][:hi]))
    b_last = [b[c][CHUNK - 1:CHUNK, :] for c in chunks]
    qd = [(q[c] * jnp.exp(b[c])).astype(BF16) for c in chunks]
    kd = [(k[c] * jnp.exp(b_last[c] - b[c])).astype(BF16) for c in chunks]
    p = [_dot(vb[c].T, kd[c]) for c in chunks]
    d = [jnp.exp(b_last[c]) for c in chunks]
    gate = nw_ref[...] * _silu(hg_ref[...])

    st = st_ref[...]
    for c in chunks:
        o = jnp.concatenate(o_intra[c], axis=0) + _dot_nt(qd[c], st.astype(BF16))
        st = st * d[c] + p[c]
        y = o * lax.rsqrt(jnp.mean(o * o, axis=-1, keepdims=True) + EPS)
        o_ref[rows[c], :] = (y * gate[rows[c]]).astype(o_ref.dtype)
    st_ref[...] = st


def hgrn_heads(proj, lb, norm_w, tb):
    t = proj.shape[1]
    nh = N_REC_HEADS

    def col(block0):
        return pl.BlockSpec((None, tb, HEAD_W), lambda h, i: (block0 + h, i, 0))

    return pl.pallas_call(
        functools.partial(_hgrn_kernel, n_chunks=tb // CHUNK),
        grid=(nh, t // tb),
        in_specs=[col(0), col(nh), col(2 * nh), col(3 * nh),
                  pl.BlockSpec((None, 1, HEAD_W), lambda h, i: (h, 0, 0)),
                  pl.BlockSpec((1, HEAD_W), lambda h, i: (0, 0))],
        out_specs=pl.BlockSpec((tb, HEAD_W), lambda h, i: (i, h)),
        out_shape=jax.ShapeDtypeStruct((t, nh * HEAD_W), BF16),
        scratch_shapes=[pltpu.VMEM((HEAD_W, HEAD_W), F32)],
        compiler_params=_cparams(("parallel", "arbitrary")),
        name="hgrn_heads",
    )(proj, proj, proj, proj, lb.reshape(nh, 1, HEAD_W), norm_w.reshape(1, HEAD_W))


def _gdn_kernel(alog_ref, dtb_ref, gq_ref, gk_ref, gv_ref, gg_ref, gab_ref, cw_ref, nw_ref, o_ref,
                s_ref, tail_ref, q_s, k_s, v_s, g_s, beta_s, u_s, wq_s, kdt_s, qk_s, gl_s, *, n_chunks):
    tb = n_chunks * CHUNK
    nh = N_REC_HEADS

    @pl.when(pl.program_id(0) == 0)
    def _():
        s_ref[...] = jnp.zeros_like(s_ref)
        tail_ref[...] = jnp.zeros_like(tail_ref)

    def conv_silu(u_ref, slot, h):
        cols = slice(h * HEAD_W, (h + 1) * HEAD_W)
        u = u_ref[h]
        ext = jnp.concatenate([tail_ref[slot, :, cols], u], axis=0)
        w = cw_ref[:, slot * nh * HEAD_W + h * HEAD_W:slot * nh * HEAD_W + (h + 1) * HEAD_W]
        acc = u * w[CONV_W - 1:CONV_W, :]
        for j in range(CONV_W - 1):
            acc = acc + pltpu.roll(ext, CONV_W - 1 - j, axis=0)[8:, :] * w[j:j + 1, :]
        tail_ref[slot, :, cols] = u[tb - 8:, :]
        return _silu(acc)

    for h in range(nh):
        qh = conv_silu(gq_ref, 0, h)
        kh = conv_silu(gk_ref, 1, h)
        q_s[h] = qh * lax.rsqrt(jnp.sum(qh * qh, axis=-1, keepdims=True) + EPS) * (HEAD_W ** -0.5)
        k_s[h] = kh * lax.rsqrt(jnp.sum(kh * kh, axis=-1, keepdims=True) + EPS)
        v_s[h] = conv_silu(gv_ref, 2, h)
    gab = gab_ref[...].T
    g = -jnp.exp(alog_ref[...]) * _softplus(gab[:nh, :] + dtb_ref[...])
    brow = lax.broadcasted_iota(jnp.int32, (tb, tb), 0)
    bcol = lax.broadcasted_iota(jnp.int32, (tb, tb), 1)
    same_chunk_upto = jnp.logical_and(brow <= bcol, brow // CHUNK == bcol // CHUNK).astype(F32)
    g_s[...] = _dot(g, same_chunk_upto, precision=HI)
    beta_s[...] = _sigmoid(gab[nh:2 * nh, :])

    row = lax.broadcasted_iota(jnp.int32, (CHUNK, CHUNK), 0)
    col = lax.broadcasted_iota(jnp.int32, (CHUNK, CHUNK), 1)
    tril = col <= row
    strict = col < row
    eye = col == row
    eye_f = eye.astype(F32)

    def wy_heads(hg, carry):
        items = [(hg * WY_GROUP + dh, c) for dh in range(WY_GROUP) for c in range(n_chunks)]
        n = range(len(items))
        rows = [slice(c * CHUNK, (c + 1) * CHUNK) for _, c in items]
        q = [q_s[h, rows[x], :] for x, (h, _) in enumerate(items)]
        k = [k_s[h, rows[x], :] for x, (h, _) in enumerate(items)]
        gc_all = [g_s[pl.ds(hg * WY_GROUP + dh, 1), :] for dh in range(WY_GROUP)]
        beta_all = [beta_s[pl.ds(hg * WY_GROUP + dh, 1), :] for dh in range(WY_GROUP)]
        gc_row = [jnp.broadcast_to(gc_all[x // n_chunks][:, rows[x]], (CHUNK, CHUNK)) for x in n]
        beta_b = [jnp.broadcast_to(beta_all[x // n_chunks][:, rows[x]], (CHUNK, CHUNK)) for x in n]
        gc_col = [jnp.sum(jnp.where(eye, gc_row[x], 0.0), axis=-1, keepdims=True) for x in n]
        beta_col = [jnp.sum(jnp.where(eye, beta_b[x], 0.0), axis=-1, keepdims=True) for x in n]
        decay = [jnp.where(tril, jnp.exp(jnp.minimum(gc_col[x] - gc_row[x], 0.0)), 0.0) for x in n]
        kb = [k[x] * beta_col[x] for x in n]
        prod = [_dot_nt(jnp.concatenate([kb[x], q[x]], axis=0).astype(BF16), k[x].astype(BF16))
                for x in n]
        n1 = [jnp.where(strict, -prod[x][:CHUNK] * decay[x], 0.0) for x in n]
        inv = [eye_f + n1[x] for x in n]
        npow = [n1[x].astype(BF16) for x in n]
        for _ in range(5):
            npow = [_dot(npow[x], npow[x]).astype(BF16) for x in n]
            inv = [inv[x] + _dot(inv[x].astype(BF16), npow[x]) for x in n]
        e_gc = [jnp.exp(gc_col[x]) for x in n]
        rhs = [jnp.concatenate([v_s[h, rows[x], :] * beta_col[x], kb[x] * e_gc[x]], axis=1)
               for x, (h, _) in enumerate(items)]
        sol = [_dot(inv[x].astype(BF16), rhs[x].astype(BF16)) for x in n]
        for x, (h, c) in enumerate(items):
            g_last = gc_col[x][CHUNK - 1:CHUNK, :]
            u_s[h, rows[x], :] = sol[x][:, :HEAD_W]
            wq_s[h, c, :CHUNK, :] = sol[x][:, HEAD_W:].astype(BF16)
            wq_s[h, c, CHUNK:, :] = (q[x] * e_gc[x]).astype(BF16)
            kdt_s[h, c] = (k[x] * jnp.exp(g_last - gc_col[x])).T.astype(BF16)
            qk_s[h, c] = (prod[x][CHUNK:] * decay[x]).astype(BF16)
            gl_s[h, c] = jnp.broadcast_to(jnp.exp(g_last), (8, HEAD_W))
        return carry

    lax.fori_loop(0, nh // WY_GROUP, wy_heads, 0)

    heads = range(nh)
    s = [s_ref[h] for h in heads]
    for c in range(n_chunks):
        lo, hi = c * CHUNK, (c + 1) * CHUNK
        ws = [_dot(wq_s[h, c], s[h].astype(BF16)) for h in heads]
        v_new = [(u_s[h, lo:hi, :] - ws[h][:CHUNK]).astype(BF16) for h in heads]
        o = [ws[h][CHUNK:] + _dot(qk_s[h, c], v_new[h]) for h in heads]
        s = [s[h] * gl_s[h, c][:1, :] + _dot(kdt_s[h, c], v_new[h]) for h in heads]
        for h in heads:
            y = o[h] * lax.rsqrt(jnp.mean(o[h] * o[h], axis=-1, keepdims=True) + EPS)
            o_ref[lo:hi, h * HEAD_W:(h + 1) * HEAD_W] = (
                y * nw_ref[...] * _silu(gg_ref[h, lo:hi, :])).astype(o_ref.dtype)
    for h in heads:
        s_ref[h] = s[h]


def gdn_heads(proj, gates, conv_w, a_log, dt_bias, norm_w, tb):
    t = proj.shape[1]
    nh = N_REC_HEADS
    gw = nh * HEAD_W
    n_chunks = tb // CHUNK

    def group(block):
        return pl.BlockSpec((nh, tb, HEAD_W), lambda i: (block, i, 0))

    def whole(a):
        return pl.BlockSpec(a.shape, lambda i: (0,) * a.ndim)

    a_log = a_log.reshape(nh, 1)
    dt_bias = dt_bias.reshape(nh, 1)
    norm_w = norm_w.reshape(1, HEAD_W)
    return pl.pallas_call(
        functools.partial(_gdn_kernel, n_chunks=n_chunks),
        grid=(t // tb,),
        in_specs=[whole(a_log), whole(dt_bias), group(4), group(5), group(6), group(7),
                  pl.BlockSpec((tb, HEAD_W), lambda i: (i, 0)), whole(conv_w), whole(norm_w)],
        out_specs=pl.BlockSpec((tb, gw), lambda i: (i, 0)),
        out_shape=jax.ShapeDtypeStruct((t, gw), BF16),
        scratch_shapes=[pltpu.VMEM((nh, HEAD_W, HEAD_W), F32),
                        pltpu.VMEM((3, 8, gw), F32),
                        pltpu.VMEM((nh, tb, HEAD_W), F32),
                        pltpu.VMEM((nh, tb, HEAD_W), F32),
                        pltpu.VMEM((nh, tb, HEAD_W), F32),
                        pltpu.VMEM((nh, tb), F32),
                        pltpu.VMEM((nh, tb), F32),
                        pltpu.VMEM((nh, tb, HEAD_W), F32),
                        pltpu.VMEM((nh, n_chunks, 2 * CHUNK, HEAD_W), BF16),
                        pltpu.VMEM((nh, n_chunks, HEAD_W, CHUNK), BF16),
                        pltpu.VMEM((nh, n_chunks, CHUNK, CHUNK), BF16),
                        pltpu.VMEM((nh, n_chunks, 8, HEAD_W), F32)],
        compiler_params=_cparams(("arbitrary",)),
        name="gdn_heads",
    )(a_log, dt_bias, proj, proj, proj, proj, gates, conv_w, norm_w)


def _sb_kernel(q_ref, k_ref, v_ref, o_ref, acc_ref, c_ref, *, n_sub):
    blk = SB_BLK
    qb0 = pl.program_id(1) * n_sub
    row = lax.broadcasted_iota(jnp.int32, (2 * blk, 2 * blk), 0)
    col = lax.broadcasted_iota(jnp.int32, (2 * blk, 2 * blk), 1)
    later2 = (row > col).astype(BF16)
    later = later2[:blk, :blk]
    causal = (col < row)[:blk, :blk]

    def tiles(subs, s, masked, watch=None, width=1):
        n = range(len(subs))
        kw = width * blk
        start = [pl.multiple_of((qb0 + i - s - (width - 1)) * blk, blk) for i in subs]
        z = [_dot_nt(q_ref[i * blk:(i + 1) * blk, :], k_ref[pl.ds(start[x], kw), :])
             for x, i in enumerate(subs)]
        sp = [_softplus(z[x].astype(BF16)).astype(F32) for x in n]
        log_1m = [jnp.where(causal, -sp[x], 0.0) if masked else -sp[x] for x in n]
        after = [_dot(log_1m[x].astype(BF16), later if width == 1 else later2)
                 for x in n]
        c = [c_ref[i] for i in subs]
        a = [jnp.exp((z[x] - sp[x]) + after[x] + c[x]) for x in n]
        if masked:
            a = [jnp.where(causal, a[x], 0.0) for x in n]
        c_new = [c[x] + after[x][:, :1] + log_1m[x][:, :1] for x in n]
        for x, i in enumerate(subs):
            acc_ref[i] += _dot(a[x].astype(BF16), v_ref[pl.ds(start[x], kw), :])
            c_ref[i] = c_new[x]
        watched = [c_new[x] for x, i in enumerate(subs) if watch is None or i in watch]
        if not watched:
            return jnp.float32(-jnp.inf)
        return jnp.max(functools.reduce(jnp.maximum, watched))

    all_subs = list(range(n_sub))
    acc_ref[...] = jnp.zeros_like(acc_ref)
    c_ref[...] = jnp.zeros_like(c_ref)
    c_max = tiles(all_subs, 0, True)

    def finish(i, s_start):
        lax.while_loop(
            lambda carry: jnp.logical_and(carry[0] <= qb0 + i, carry[1] > EXP_ZERO_BELOW),
            lambda carry: (carry[0] + 1, tiles([i], carry[0], False)),
            (s_start, jnp.max(c_ref[i])))

    @pl.when(qb0 > 0)
    def _():
        s_end, c_left = lax.while_loop(
            lambda carry: jnp.logical_and(carry[0] + (SB_SWEEP - 1) <= qb0, carry[1] > EXP_ZERO_BELOW),
            lambda carry: (carry[0] + SB_SWEEP, tiles(all_subs, carry[0], False, width=SB_SWEEP)),
            (jnp.int32(1), c_max))

        @pl.when(c_left > EXP_ZERO_BELOW)
        def _():
            for i in range(1, n_sub):
                finish(i, s_end)

    def first_block_sweep(s, c_left):
        if s < n_sub:
            @pl.when(c_left > EXP_ZERO_BELOW)
            def _():
                subs = list(range(s, n_sub))
                if s <= SB_FIRST_UNROLL:
                    first_block_sweep(s + 1, tiles(subs, s, False, watch=subs[1:]))
                else:
                    for i in subs:
                        finish(i, jnp.int32(s))

    @pl.when(qb0 == 0)
    def _():
        first_block_sweep(1, c_max)

    for i in range(n_sub):
        o_ref[i * blk:(i + 1) * blk, :] = acc_ref[i].astype(o_ref.dtype)


def sb_attention(qkv, n_heads, n_sub):
    t = qkv.shape[1]
    qb = n_sub * SB_BLK
    return pl.pallas_call(
        functools.partial(_sb_kernel, n_sub=n_sub),
        grid=(n_heads, t // qb),
        in_specs=[pl.BlockSpec((None, qb, HEAD_W), lambda h, i: (h, i, 0)),
                  pl.BlockSpec((None, t, HEAD_W), lambda h, i: (n_heads + h, 0, 0)),
                  pl.BlockSpec((None, t, HEAD_W), lambda h, i: (2 * n_heads + h, 0, 0))],
        out_specs=pl.BlockSpec((qb, HEAD_W), lambda h, i: (i, h)),
        out_shape=jax.ShapeDtypeStruct((t, n_heads * HEAD_W), BF16),
        scratch_shapes=[pltpu.VMEM((n_sub, SB_BLK, HEAD_W), F32),
                        pltpu.VMEM((n_sub, SB_BLK, 1), F32)],
        compiler_params=_cparams(("parallel", "arbitrary")),
        name="sb_attention",
    )(qkv, qkv, qkv)


def _pick(n, candidates):
    for c in candidates:
        if n % c == 0:
            return c
    return n


def kernel(x, mix_norm, a_w_in, a_conv_w, a_a_log, a_dt_bias, a_lb_logits, a_hgrn_norm,
           a_gdn_norm, a_w_out, c_w_qkv, c_w_o, mlp_norm, mlp_w1, mlp_w2, final_norm):
    bsz, t, d = x.shape
    assert bsz == 1
    depth = mix_norm.shape[0]
    nh = N_REC_HEADS
    kw = nh * HEAD_W
    xs = x.reshape(t, d)
    tm = _pick(t, (1024, 512, 256, 128, 64))
    tm_proj = _pick(t, (512, 256, 128, 64))
    tn_in = _pick(kw, (1024, 512, 256, 128))
    tb_hgrn = _pick(t, (2048, 1024, 512, 256, 128, 64))
    tb_gdn = _pick(t, (256, 128, 64))
    sb_heads = d // HEAD_W
    sb_sub = _pick(t // SB_BLK, (16, 8, 4, 2, 1))

    lb_all = jnp.cumsum(jax.nn.softmax(a_lb_logits.astype(F32), axis=0), axis=0)
    mlp_w1_b, mlp_w2_b = mlp_w1.astype(BF16), mlp_w2.astype(BF16)
    for layer in range(depth):
        j = layer // 2
        if layer % 2 == 0:
            w_in = a_w_in[j]
            n_gate = 2 * nh
            w_in_b = w_in.astype(BF16)
            w_gate = jnp.pad(w_in_b[:, 7 * kw:7 * kw + n_gate], ((0, 0), (0, HEAD_W - n_gate)))
            w_parts = [(w_in_b, 7 * kw // tn_in), (w_in_b[:, 7 * kw + n_gate:], kw // tn_in)]
            proj, gates = norm_matmul(xs, mix_norm[layer], w_parts, jnp.ones((8 * kw,), F32), F32,
                                      tm, tn_in, w_extra=w_gate)
            o_a = hgrn_heads(proj, lb_all[j], a_hgrn_norm[j], tb_hgrn)
            o_b = gdn_heads(proj, gates, a_conv_w[j], a_a_log[j], a_dt_bias[j], a_gdn_norm[j], tb_gdn)
            w_out = a_w_out[j].astype(BF16)
            xs = proj_residual([o_a, o_b], [w_out[:kw], w_out[kw:]], xs, tm_proj, d)
        else:
            scale = jnp.concatenate([jnp.full((d,), HEAD_W ** -0.5, F32), jnp.ones((2 * d,), F32)])
            qkv, = norm_matmul(xs, mix_norm[layer], [(c_w_qkv[j].astype(BF16), 3 * d // tn_in)], scale, BF16,
                               tm, tn_in)
            o_c = sb_attention(qkv, sb_heads, sb_sub)
            xs = proj_residual([o_c], [c_w_o[j].astype(BF16)], xs, tm_proj, d)
        last = layer == depth - 1
        xs = mlp_residual(xs, mlp_norm[layer], mlp_w1_b, mlp_w2_b, layer,
                          final_norm, last, tm, _pick(mlp_w1.shape[2], (512,)))
    return xs.reshape(bsz, t, d)
```

```python
import functools

import jax
import jax.numpy as jnp
from jax import lax
from jax.experimental import pallas as pl
from jax.experimental.pallas import tpu as pltpu

F32 = jnp.float32
BF16 = jnp.bfloat16
EPS = 1e-6
HEAD_W = 128
CHUNK = 64
SUB = 16
CONV_W = 4
SB_BLK = 128
SB_SWEEP = 2
SB_FIRST_UNROLL = 2
N_REC_HEADS = 8
WY_GROUP = 8
VMEM_LIMIT = 56 * 1024 * 1024
HI = lax.Precision.HIGHEST
EXP_ZERO_BELOW = -104.0


def _cparams(sem):
    return pltpu.CompilerParams(dimension_semantics=sem, vmem_limit_bytes=VMEM_LIMIT)


def _dot(a, b, precision=None):
    return jnp.dot(a, b, preferred_element_type=F32, precision=precision)


def _dot_nt(a, b, precision=None):
    return lax.dot_general(a, b, (((1,), (1,)), ((), ())),
                           preferred_element_type=F32, precision=precision)


def _rms(x, gain):
    return x * lax.rsqrt(jnp.mean(x * x, axis=-1, keepdims=True) + EPS) * gain


def _sigmoid(x):
    return 0.5 * jnp.tanh(0.5 * x) + 0.5


def _silu(x):
    return x * _sigmoid(x)


def _softplus(x):
    return jnp.maximum(x, 0.0) + jnp.log(1.0 + jnp.exp2(jnp.abs(x) * -1.4426950408889634))


def _norm_matmul_kernel(*refs, parts, has_extra):
    n_parts = len(parts)
    x_ref, g_ref = refs[:2]
    w_refs = refs[2:2 + n_parts]
    rest = refs[2 + n_parts:]
    if has_extra:
        cs_ref, we_ref, o_ref, oe_ref, h_ref = rest
    else:
        cs_ref, o_ref, h_ref = rest
    j = pl.program_id(1)

    @pl.when(j == 0)
    def _():
        h_ref[...] = _rms(x_ref[...], g_ref[...]).astype(BF16)
        if has_extra:
            oe_ref[...] = _dot(h_ref[...], we_ref[...])

    def emit(w_ref):
        acc = _dot(h_ref[...], w_ref[...]) * cs_ref[...]
        for c in range(o_ref.shape[0]):
            o_ref[c] = acc[:, c * HEAD_W:(c + 1) * HEAD_W].astype(o_ref.dtype)

    if n_parts == 1:
        emit(w_refs[0])
    else:
        for w_ref, (lo, hi) in zip(w_refs, parts):
            pl.when(jnp.logical_and(j >= lo, j < hi))(functools.partial(emit, w_ref))


def norm_matmul(x, gain, w_parts, col_scale, out_dtype, tm, tn, w_extra=None):
    m, k = x.shape
    has_extra = w_extra is not None
    parts, lo = [], 0
    for _, n_tiles in w_parts:
        parts.append((lo, lo + n_tiles))
        lo += n_tiles
    n = lo * tn
    in_specs = [pl.BlockSpec((tm, k), lambda i, j: (i, 0)),
                pl.BlockSpec((1, k), lambda i, j: (0, 0))]
    for lo, hi in parts:
        in_specs.append(pl.BlockSpec((k, tn), lambda i, j, lo=lo, hi=hi: (0, jnp.clip(j - lo, 0, hi - lo - 1))))
    in_specs.append(pl.BlockSpec((1, tn), lambda i, j: (0, j)))
    out_specs = [pl.BlockSpec((tn // HEAD_W, tm, HEAD_W), lambda i, j: (j, i, 0))]
    out_shape = [jax.ShapeDtypeStruct((n // HEAD_W, m, HEAD_W), out_dtype)]
    args = [x, gain.reshape(1, k)] + [w for w, _ in w_parts] + [col_scale.reshape(1, n)]
    if has_extra:
        in_specs.append(pl.BlockSpec((k, HEAD_W), lambda i, j: (0, 0)))
        out_specs.append(pl.BlockSpec((tm, HEAD_W), lambda i, j: (i, 0)))
        out_shape.append(jax.ShapeDtypeStruct((m, HEAD_W), F32))
        args.append(w_extra)
    return pl.pallas_call(
        functools.partial(_norm_matmul_kernel, parts=tuple(parts), has_extra=has_extra),
        grid=(m // tm, n // tn),
        in_specs=in_specs,
        out_specs=out_specs,
        out_shape=out_shape,
        scratch_shapes=[pltpu.VMEM((tm, k), BF16)],
        compiler_params=_cparams(("parallel", "arbitrary")),
        name="norm_matmul",
    )(*args)


def _proj_residual_kernel(*refs, n_lhs):
    lhs = refs[:n_lhs]
    ws = refs[n_lhs:2 * n_lhs]
    x_ref, o_ref = refs[2 * n_lhs], refs[2 * n_lhs + 1]
    acc = x_ref[...]
    for a_ref, w_ref in zip(lhs, ws):
        acc = acc + _dot(a_ref[...], w_ref[...])
    o_ref[...] = acc


def proj_residual(lhs_list, w_list, x, tm, tn):
    m, n = x.shape
    n_lhs = len(lhs_list)
    in_specs = [pl.BlockSpec((tm, a.shape[1]), lambda i, j: (i, 0)) for a in lhs_list]
    in_specs += [pl.BlockSpec((w.shape[0], tn), lambda i, j: (0, j)) for w in w_list]
    in_specs += [pl.BlockSpec((tm, tn), lambda i, j: (i, j))]
    return pl.pallas_call(
        functools.partial(_proj_residual_kernel, n_lhs=n_lhs),
        grid=(m // tm, n // tn),
        in_specs=in_specs,
        out_specs=pl.BlockSpec((tm, tn), lambda i, j: (i, j)),
        out_shape=jax.ShapeDtypeStruct((m, n), F32),
        compiler_params=_cparams(("parallel", "arbitrary")),
        name="proj_residual",
    )(*lhs_list, *w_list, x)


def _mlp_kernel(x_ref, g_ref, w1_ref, w2_ref, fg_ref, o_ref, h_ref, *, final_norm):
    j = pl.program_id(1)

    @pl.when(j == 0)
    def _():
        x = x_ref[...]
        h_ref[...] = _rms(x, g_ref[...]).astype(BF16)
        o_ref[...] = x

    a = _dot(h_ref[...], w1_ref[...])
    a = jnp.square(jnp.maximum(a, 0.0)).astype(BF16)
    o_ref[...] += _dot(a, w2_ref[...])

    if final_norm:
        @pl.when(j == pl.num_programs(1) - 1)
        def _():
            o_ref[...] = _rms(o_ref[...], fg_ref[...])


def mlp_residual(x, gain, w1, w2, layer, final_gain, final_norm, tm, tf):
    m, d = x.shape
    ff = w1.shape[2]
    return pl.pallas_call(
        functools.partial(_mlp_kernel, final_norm=final_norm),
        grid=(m // tm, ff // tf),
        in_specs=[
            pl.BlockSpec((tm, d), lambda i, j: (i, 0)),
            pl.BlockSpec((1, d), lambda i, j: (0, 0)),
            pl.BlockSpec((None, d, tf), lambda i, j: (layer, 0, j)),
            pl.BlockSpec((None, tf, d), lambda i, j: (layer, j, 0)),
            pl.BlockSpec((1, d), lambda i, j: (0, 0)),
        ],
        out_specs=pl.BlockSpec((tm, d), lambda i, j: (i, 0)),
        out_shape=jax.ShapeDtypeStruct((m, d), F32),
        scratch_shapes=[pltpu.VMEM((tm, d), BF16)],
        compiler_params=_cparams(("parallel", "arbitrary")),
        name="mlp_residual",
    )(x, gain.reshape(1, d), w1, w2, final_gain.reshape(1, d))


def _hgrn_kernel(hq_ref, hf_ref, hi_ref, hg_ref, lb_ref, nw_ref, o_ref, st_ref, *, n_chunks):
    @pl.when(pl.program_id(1) == 0)
    def _():
        st_ref[...] = jnp.zeros_like(st_ref)

    chunks = range(n_chunks)
    n_sub = CHUNK // SUB
    lb = lb_ref[...]
    row = lax.broadcasted_iota(jnp.int32, (CHUNK, CHUNK), 0)
    col = lax.broadcasted_iota(jnp.int32, (CHUNK, CHUNK), 1)
    tril = (col <= row).astype(F32)

    hf = hf_ref[...]
    log_f = jnp.log(lb + (1.0 - lb) * _sigmoid(hf))
    k_all = (1.0 - lb) * _sigmoid(-hf)
    q_all = _silu(hq_ref[...])
    vb_all = hi_ref[...].astype(BF16)

    rows = [slice(c * CHUNK, (c + 1) * CHUNK) for c in chunks]
    q = [q_all[r] for r in rows]
    k = [k_all[r] for r in rows]
    vb = [vb_all[r] for r in rows]
    b = [_dot(tril, log_f[r], precision=HI) for r in rows]
    o_intra = [[] for _ in chunks]
    for i in range(n_sub):
        lo, hi = i * SUB, (i + 1) * SUB
        srow = lax.broadcasted_iota(jnp.int32, (SUB, hi), 0)
        scol = lax.broadcasted_iota(jnp.int32, (SUB, hi), 1)
        causal = scol <= srow + lo
        b_ref = [b[c][lo:lo + 1, :] for c in chunks]
        qs = [(q[c][lo:hi] * jnp.exp(b[c][lo:hi] - b_ref[c])).astype(BF16) for c in chunks]
        ks = [(k[c][:hi] * jnp.exp(b_ref[c] - b[c][:hi])).astype(BF16) for c in chunks]
        att = [jnp.where(causal, _dot_nt(qs[c], ks[c]), 0.0).astype(BF16) for c in chunks]
        for c in chunks:
            o_intra[c].append(_dot(att[c], vb[c][:hi]))
    b_last = [b[c][CHUNK - 1:CHUNK, :] for c in chunks]
    qd = [(q[c] * jnp.exp(b[c])).astype(BF16) for c in chunks]
    kd = [(k[c] * jnp.exp(b_last[c] - b[c])).astype(BF16) for c in chunks]
    p = [_dot(vb[c].T, kd[c]) for c in chunks]
    d = [jnp.exp(b_last[c]) for c in chunks]
    gate = nw_ref[...] * _silu(hg_ref[...])

    st = st_ref[...]
    for c in chunks:
        o = jnp.concatenate(o_intra[c], axis=0) + _dot_nt(qd[c], st.astype(BF16))
        st = st * d[c] + p[c]
        y = o * lax.rsqrt(jnp.mean(o * o, axis=-1, keepdims=True) + EPS)
        o_ref[rows[c], :] = (y * gate[rows[c]]).astype(o_ref.dtype)
    st_ref[...] = st


def hgrn_heads(proj, lb, norm_w, tb):
    t = proj.shape[1]
    nh = N_REC_HEADS

    def col(block0):
        return pl.BlockSpec((None, tb, HEAD_W), lambda h, i: (block0 + h, i, 0))

    return pl.pallas_call(
        functools.partial(_hgrn_kernel, n_chunks=tb // CHUNK),
        grid=(nh, t // tb),
        in_specs=[col(0), col(nh), col(2 * nh), col(3 * nh),
                  pl.BlockSpec((None, 1, HEAD_W), lambda h, i: (h, 0, 0)),
                  pl.BlockSpec((1, HEAD_W), lambda h, i: (0, 0))],
        out_specs=pl.BlockSpec((tb, HEAD_W), lambda h, i: (i, h)),
        out_shape=jax.ShapeDtypeStruct((t, nh * HEAD_W), BF16),
        scratch_shapes=[pltpu.VMEM((HEAD_W, HEAD_W), F32)],
        compiler_params=_cparams(("parallel", "arbitrary")),
        name="hgrn_heads",
    )(proj, proj, proj, proj, lb.reshape(nh, 1, HEAD_W), norm_w.reshape(1, HEAD_W))


def _gdn_kernel(alog_ref, dtb_ref, gq_ref, gk_ref, gv_ref, gg_ref, gab_ref, cw_ref, nw_ref, o_ref,
                s_ref, tail_ref, q_s, k_s, v_s, g_s, beta_s, u_s, wq_s, kdt_s, qk_s, gl_s, *, n_chunks):
    tb = n_chunks * CHUNK
    nh = N_REC_HEADS

    @pl.when(pl.program_id(0) == 0)
    def _():
        s_ref[...] = jnp.zeros_like(s_ref)
        tail_ref[...] = jnp.zeros_like(tail_ref)

    def conv_silu(u_ref, slot, h):
        cols = slice(h * HEAD_W, (h + 1) * HEAD_W)
        u = u_ref[h]
        ext = jnp.concatenate([tail_ref[slot, :, cols], u], axis=0)
        w = cw_ref[:, slot * nh * HEAD_W + h * HEAD_W:slot * nh * HEAD_W + (h + 1) * HEAD_W]
        acc = u * w[CONV_W - 1:CONV_W, :]
        for j in range(CONV_W - 1):
            acc = acc + pltpu.roll(ext, CONV_W - 1 - j, axis=0)[8:, :] * w[j:j + 1, :]
        tail_ref[slot, :, cols] = u[tb - 8:, :]
        return _silu(acc)

    for h in range(nh):
        qh = conv_silu(gq_ref, 0, h)
        kh = conv_silu(gk_ref, 1, h)
        q_s[h] = qh * lax.rsqrt(jnp.sum(qh * qh, axis=-1, keepdims=True) + EPS) * (HEAD_W ** -0.5)
        k_s[h] = kh * lax.rsqrt(jnp.sum(kh * kh, axis=-1, keepdims=True) + EPS)
        v_s[h] = conv_silu(gv_ref, 2, h)
    gab = gab_ref[...].T
    g = -jnp.exp(alog_ref[...]) * _softplus(gab[:nh, :] + dtb_ref[...])
    brow = lax.broadcasted_iota(jnp.int32, (tb, tb), 0)
    bcol = lax.broadcasted_iota(jnp.int32, (tb, tb), 1)
    same_chunk_upto = jnp.logical_and(brow <= bcol, brow // CHUNK == bcol // CHUNK).astype(F32)
    g_s[...] = _dot(g, same_chunk_upto, precision=HI)
    beta_s[...] = _sigmoid(gab[nh:2 * nh, :])

    row = lax.broadcasted_iota(jnp.int32, (CHUNK, CHUNK), 0)
    col = lax.broadcasted_iota(jnp.int32, (CHUNK, CHUNK), 1)
    tril = col <= row
    strict = col < row
    eye = col == row
    eye_f = eye.astype(F32)

    def wy_heads(hg, carry):
        items = [(hg * WY_GROUP + dh, c) for dh in range(WY_GROUP) for c in range(n_chunks)]
        n = range(len(items))
        rows = [slice(c * CHUNK, (c + 1) * CHUNK) for _, c in items]
        q = [q_s[h, rows[x], :] for x, (h, _) in enumerate(items)]
        k = [k_s[h, rows[x], :] for x, (h, _) in enumerate(items)]
        gc_all = [g_s[pl.ds(hg * WY_GROUP + dh, 1), :] for dh in range(WY_GROUP)]
        beta_all = [beta_s[pl.ds(hg * WY_GROUP + dh, 1), :] for dh in range(WY_GROUP)]
        gc_row = [jnp.broadcast_to(gc_all[x // n_chunks][:, rows[x]], (CHUNK, CHUNK)) for x in n]
        beta_b = [jnp.broadcast_to(beta_all[x // n_chunks][:, rows[x]], (CHUNK, CHUNK)) for x in n]
        gc_col = [jnp.sum(jnp.where(eye, gc_row[x], 0.0), axis=-1, keepdims=True) for x in n]
        beta_col = [jnp.sum(jnp.where(eye, beta_b[x], 0.0), axis=-1, keepdims=True) for x in n]
        decay = [jnp.where(tril, jnp.exp(jnp.minimum(gc_col[x] - gc_row[x], 0.0)), 0.0) for x in n]
        kb = [k[x] * beta_col[x] for x in n]
        prod = [_dot_nt(jnp.concatenate([kb[x], q[x]], axis=0).astype(BF16), k[x].astype(BF16))
                for x in n]
        n1 = [jnp.where(strict, -prod[x][:CHUNK] * decay[x], 0.0) for x in n]
        inv = [eye_f + n1[x] for x in n]
        npow = [n1[x].astype(BF16) for x in n]
        for _ in range(5):
            npow = [_dot(npow[x], npow[x]).astype(BF16) for x in n]
            inv = [inv[x] + _dot(inv[x].astype(BF16), npow[x]) for x in n]
        e_gc = [jnp.exp(gc_col[x]) for x in n]
        rhs = [jnp.concatenate([v_s[h, rows[x], :] * beta_col[x], kb[x] * e_gc[x]], axis=1)
               for x, (h, _) in enumerate(items)]
        sol = [_dot(inv[x].astype(BF16), rhs[x].astype(BF16)) for x in n]
        for x, (h, c) in enumerate(items):
            g_last = gc_col[x][CHUNK - 1:CHUNK, :]
            u_s[h, rows[x], :] = sol[x][:, :HEAD_W]
            wq_s[h, c, :CHUNK, :] = sol[x][:, HEAD_W:].astype(BF16)
            wq_s[h, c, CHUNK:, :] = (q[x] * e_gc[x]).astype(BF16)
            kdt_s[h, c] = (k[x] * jnp.exp(g_last - gc_col[x])).T.astype(BF16)
            qk_s[h, c] = (prod[x][CHUNK:] * decay[x]).astype(BF16)
            gl_s[h, c] = jnp.broadcast_to(jnp.exp(g_last), (8, HEAD_W))
        return carry

    lax.fori_loop(0, nh // WY_GROUP, wy_heads, 0)

    heads = range(nh)
    s = [s_ref[h] for h in heads]
    for c in range(n_chunks):
        lo, hi = c * CHUNK, (c + 1) * CHUNK
        ws = [_dot(wq_s[h, c], s[h].astype(BF16)) for h in heads]
        v_new = [(u_s[h, lo:hi, :] - ws[h][:CHUNK]).astype(BF16) for h in heads]
        o = [ws[h][CHUNK:] + _dot(qk_s[h, c], v_new[h]) for h in heads]
        s = [s[h] * gl_s[h, c][:1, :] + _dot(kdt_s[h, c], v_new[h]) for h in heads]
        for h in heads:
            y = o[h] * lax.rsqrt(jnp.mean(o[h] * o[h], axis=-1, keepdims=True) + EPS)
            o_ref[lo:hi, h * HEAD_W:(h + 1) * HEAD_W] = (
                y * nw_ref[...] * _silu(gg_ref[h, lo:hi, :])).astype(o_ref.dtype)
    for h in heads:
        s_ref[h] = s[h]


def gdn_heads(proj, gates, conv_w, a_log, dt_bias, norm_w, tb):
    t = proj.shape[1]
    nh = N_REC_HEADS
    gw = nh * HEAD_W
    n_chunks = tb // CHUNK

    def group(block):
        return pl.BlockSpec((nh, tb, HEAD_W), lambda i: (block, i, 0))

    def whole(a):
        return pl.BlockSpec(a.shape, lambda i: (0,) * a.ndim)

    a_log = a_log.reshape(nh, 1)
    dt_bias = dt_bias.reshape(nh, 1)
    norm_w = norm_w.reshape(1, HEAD_W)
    return pl.pallas_call(
        functools.partial(_gdn_kernel, n_chunks=n_chunks),
        grid=(t // tb,),
        in_specs=[whole(a_log), whole(dt_bias), group(4), group(5), group(6), group(7),
                  pl.BlockSpec((tb, HEAD_W), lambda i: (i, 0)), whole(conv_w), whole(norm_w)],
        out_specs=pl.BlockSpec((tb, gw), lambda i: (i, 0)),
        out_shape=jax.ShapeDtypeStruct((t, gw), BF16),
        scratch_shapes=[pltpu.VMEM((nh, HEAD_W, HEAD_W), F32),
                        pltpu.VMEM((3, 8, gw), F32),
                        pltpu.VMEM((nh, tb, HEAD_W), F32),
                        pltpu.VMEM((nh, tb, HEAD_W), F32),
                        pltpu.VMEM((nh, tb, HEAD_W), F32),
                        pltpu.VMEM((nh, tb), F32),
                        pltpu.VMEM((nh, tb), F32),
                        pltpu.VMEM((nh, tb, HEAD_W), F32),
                        pltpu.VMEM((nh, n_chunks, 2 * CHUNK, HEAD_W), BF16),
                        pltpu.VMEM((nh, n_chunks, HEAD_W, CHUNK), BF16),
                        pltpu.VMEM((nh, n_chunks, CHUNK, CHUNK), BF16),
                        pltpu.VMEM((nh, n_chunks, 8, HEAD_W), F32)],
        compiler_params=_cparams(("arbitrary",)),
        name="gdn_heads",
    )(a_log, dt_bias, proj, proj, proj, proj, gates, conv_w, norm_w)


def _sb_kernel(q_ref, k_ref, v_ref, o_ref, acc_ref, c_ref, *, n_sub):
    blk = SB_BLK
    qb0 = pl.program_id(1) * n_sub
    row = lax.broadcasted_iota(jnp.int32, (2 * blk, 2 * blk), 0)
    col = lax.broadcasted_iota(jnp.int32, (2 * blk, 2 * blk), 1)
    later2 = (row > col).astype(BF16)
    later = later2[:blk, :blk]
    causal = (col < row)[:blk, :blk]

    def tiles(subs, s, masked, watch=None, width=1):
        n = range(len(subs))
        kw = width * blk
        start = [pl.multiple_of((qb0 + i - s - (width - 1)) * blk, blk) for i in subs]
        z = [_dot_nt(q_ref[i * blk:(i + 1) * blk, :], k_ref[pl.ds(start[x], kw), :])
             for x, i in enumerate(subs)]
        sp = [_softplus(z[x]) for x in n]
        log_1m = [jnp.where(causal, -sp[x], 0.0) if masked else -sp[x] for x in n]
        after = [_dot(log_1m[x].astype(BF16), later if width == 1 else later2)
                 for x in n]
        c = [c_ref[i] for i in subs]
        a = [jnp.exp((z[x] - sp[x]) + after[x] + c[x]) for x in n]
        if masked:
            a = [jnp.where(causal, a[x], 0.0) for x in n]
        c_new = [c[x] + after[x][:, :1] + log_1m[x][:, :1] for x in n]
        for x, i in enumerate(subs):
            acc_ref[i] += _dot(a[x].astype(BF16), v_ref[pl.ds(start[x], kw), :])
            c_ref[i] = c_new[x]
        watched = [c_new[x] for x, i in enumerate(subs) if watch is None or i in watch]
        if not watched:
            return jnp.float32(-jnp.inf)
        return jnp.max(functools.reduce(jnp.maximum, watched))

    all_subs = list(range(n_sub))
    acc_ref[...] = jnp.zeros_like(acc_ref)
    c_ref[...] = jnp.zeros_like(c_ref)
    c_max = tiles(all_subs, 0, True)

    def finish(i, s_start):
        lax.while_loop(
            lambda carry: jnp.logical_and(carry[0] <= qb0 + i, carry[1] > EXP_ZERO_BELOW),
            lambda carry: (carry[0] + 1, tiles([i], carry[0], False)),
            (s_start, jnp.max(c_ref[i])))

    @pl.when(qb0 > 0)
    def _():
        s_end, c_left = lax.while_loop(
            lambda carry: jnp.logical_and(carry[0] + (SB_SWEEP - 1) <= qb0, carry[1] > EXP_ZERO_BELOW),
            lambda carry: (carry[0] + SB_SWEEP, tiles(all_subs, carry[0], False, width=SB_SWEEP)),
            (jnp.int32(1), c_max))

        @pl.when(c_left > EXP_ZERO_BELOW)
        def _():
            for i in range(1, n_sub):
                finish(i, s_end)

    def first_block_sweep(s, c_left):
        if s < n_sub:
            @pl.when(c_left > EXP_ZERO_BELOW)
            def _():
                subs = list(range(s, n_sub))
                if s <= SB_FIRST_UNROLL:
                    first_block_sweep(s + 1, tiles(subs, s, False, watch=subs[1:]))
                else:
                    for i in subs:
                        finish(i, jnp.int32(s))

    @pl.when(qb0 == 0)
    def _():
        first_block_sweep(1, c_max)

    for i in range(n_sub):
        o_ref[i * blk:(i + 1) * blk, :] = acc_ref[i].astype(o_ref.dtype)


def sb_attention(qkv, n_heads, n_sub):
    t = qkv.shape[1]
    qb = n_sub * SB_BLK
    return pl.pallas_call(
        functools.partial(_sb_kernel, n_sub=n_sub),
        grid=(n_heads, t // qb),
        in_specs=[pl.BlockSpec((None, qb, HEAD_W), lambda h, i: (h, i, 0)),
                  pl.BlockSpec((None, t, HEAD_W), lambda h, i: (n_heads + h, 0, 0)),
                  pl.BlockSpec((None, t, HEAD_W), lambda h, i: (2 * n_heads + h, 0, 0))],
        out_specs=pl.BlockSpec((qb, HEAD_W), lambda h, i: (i, h)),
        out_shape=jax.ShapeDtypeStruct((t, n_heads * HEAD_W), BF16),
        scratch_shapes=[pltpu.VMEM((n_sub, SB_BLK, HEAD_W), F32),
                        pltpu.VMEM((n_sub, SB_BLK, 1), F32)],
        compiler_params=_cparams(("parallel", "arbitrary")),
        name="sb_attention",
    )(qkv, qkv, qkv)


def _pick(n, candidates):
    for c in candidates:
        if n % c == 0:
            return c
    return n


def kernel(x, mix_norm, a_w_in, a_conv_w, a_a_log, a_dt_bias, a_lb_logits, a_hgrn_norm,
           a_gdn_norm, a_w_out, c_w_qkv, c_w_o, mlp_norm, mlp_w1, mlp_w2, final_norm):
    bsz, t, d = x.shape
    assert bsz == 1
    depth = mix_norm.shape[0]
    nh = N_REC_HEADS
    kw = nh * HEAD_W
    xs = x.reshape(t, d)
    tm = _pick(t, (1024, 512, 256, 128, 64))
    tm_proj = _pick(t, (512, 256, 128, 64))
    tn_in = _pick(kw, (1024, 512, 256, 128))
    tb_hgrn = _pick(t, (2048, 1024, 512, 256, 128, 64))
    tb_gdn = _pick(t, (256, 128, 64))
    sb_heads = d // HEAD_W
    sb_sub = _pick(t // SB_BLK, (16, 8, 4, 2, 1))

    lb_all = jnp.cumsum(jax.nn.softmax(a_lb_logits.astype(F32), axis=0), axis=0)
    mlp_w1_b, mlp_w2_b = mlp_w1.astype(BF16), mlp_w2.astype(BF16)
    for layer in range(depth):
        j = layer // 2
        if layer % 2 == 0:
            w_in = a_w_in[j]
            n_gate = 2 * nh
            w_in_b = w_in.astype(BF16)
            w_gate = jnp.pad(w_in_b[:, 7 * kw:7 * kw + n_gate], ((0, 0), (0, HEAD_W - n_gate)))
            w_parts = [(w_in_b, 7 * kw // tn_in), (w_in_b[:, 7 * kw + n_gate:], kw // tn_in)]
            proj, gates = norm_matmul(xs, mix_norm[layer], w_parts, jnp.ones((8 * kw,), F32), F32,
                                      tm, tn_in, w_extra=w_gate)
            o_a = hgrn_heads(proj, lb_all[j], a_hgrn_norm[j], tb_hgrn)
            o_b = gdn_heads(proj, gates, a_conv_w[j], a_a_log[j], a_dt_bias[j], a_gdn_norm[j], tb_gdn)
            w_out = a_w_out[j].astype(BF16)
            xs = proj_residual([o_a, o_b], [w_out[:kw], w_out[kw:]], xs, tm_proj, d)
        else:
            scale = jnp.concatenate([jnp.full((d,), HEAD_W ** -0.5, F32), jnp.ones((2 * d,), F32)])
            qkv, = norm_matmul(xs, mix_norm[layer], [(c_w_qkv[j].astype(BF16), 3 * d // tn_in)], scale, BF16,
                               tm, tn_in)
            o_c = sb_attention(qkv, sb_heads, sb_sub)
            xs = proj_residual([o_c], [c_w_o[j].astype(BF16)], xs, tm_proj, d)
        last = layer == depth - 1
        xs = mlp_residual(xs, mlp_norm[layer], mlp_w1_b, mlp_w2_b, layer,
                          final_norm, last, tm, _pick(mlp_w1.shape[2], (512,)))
    return xs.reshape(bsz, t, d)
```
